```python
import math
import jax
import jax.numpy as jnp
from jax import lax
import numpy as np

D_MODEL = 2048
BATCH = 4
SEQ = 2048
DEPTH = 4
DEC_BATCH = 128
DEC_SEQ = 8
PAST_LEN = 8192
PAGE_SIZE = 128

N_MIXERS = 4
N_HEADS = 16
HEAD_DIM = D_MODEL // N_HEADS
N_BUCKETS = 32
MAX_DISTANCE = 128
RMS_EPS = 1e-6
NEG_INF = -1e30
POOL_NUM = 5
POOL_DEN = 4

GLA_HEADS = 4
GLA_DK = D_MODEL // 2 // GLA_HEADS
GLA_DV = D_MODEL // GLA_HEADS
GLA_GATE_RANK = 16
GLA_TAU = 16.0
GLA_CHUNK = 64

NSA_KV_HEADS = 2
NSA_CMP_BLOCK = 32
NSA_CMP_STRIDE = 16
NSA_SEL_BLOCK = 64
NSA_N_SEL = 16
NSA_WINDOW = 512
NSA_QBLOCK = 64
NSA_FORCE = 1e9

MLA_Q_RANK = 512
MLA_KV_RANK = 512
MLA_NOPE = 128
MLA_ROPE = 64
MLA_V = 128
ROPE_THETA = 10000.0
MLA_QBLOCK = 128

MOBA_KV_HEADS = 4
MOBA_BLOCK = 256
MOBA_TOPK = 3
MOBA_QBLOCK = 32

D_FF = 5632
CONV_W = 3

kernel_name = 'hybrid_gla_nsa_mla_moba_decoder_step'


def _layers_of(m):
    return len(range(m, DEPTH, N_MIXERS))


def rms_norm(x, g):
    xf = x.astype(jnp.float32)
    y = xf * lax.rsqrt(jnp.mean(xf * xf, axis=-1, keepdims=True) + RMS_EPS)
    return (y * g.astype(jnp.float32)).astype(x.dtype)


def rel_bucket(dist):
    n = jnp.maximum(dist, 0)
    exact = N_BUCKETS // 2
    lg = jnp.log(jnp.maximum(n, 1).astype(jnp.float32) / exact) / math.log(MAX_DISTANCE / exact)
    large = jnp.minimum(exact + (lg * (N_BUCKETS - exact)).astype(jnp.int32), N_BUCKETS - 1)
    return jnp.where(n < exact, n, large)


def shared_key_bias(dist, rel_bias, groups):
    nq, nk = dist.shape
    b = rel_bias[rel_bucket(dist)].astype(jnp.float32)
    return b.reshape(nq, nk, groups, -1).transpose(0, 2, 3, 1)


def masked_softmax(logits, mask):
    z = jnp.where(mask, logits, NEG_INF)
    z = z - jnp.max(z, axis=-1, keepdims=True)
    e = jnp.where(mask, jnp.exp(z), 0.0)
    return e / jnp.maximum(jnp.sum(e, axis=-1, keepdims=True), 1e-30)


def gather_pages(pool, layer, pages):
    rows = pool[layer, pages]
    return rows.reshape((-1,) + pool.shape[3:])


def apply_rope(x, pos):
    half = MLA_ROPE // 2
    inv = ROPE_THETA ** (-jnp.arange(half, dtype=jnp.float32) / half)
    ang = pos.astype(jnp.float32)[:, None] * inv
    shape = (ang.shape[0],) + (1,) * (x.ndim - 3) + (half,)
    cos, sin = jnp.cos(ang).reshape(shape), jnp.sin(ang).reshape(shape)
    x1, x2 = x[..., :half].astype(jnp.float32), x[..., half:].astype(jnp.float32)
    return jnp.concatenate([x1 * cos - x2 * sin, x2 * cos + x1 * sin], axis=-1).astype(x.dtype)


def gla_scan(q, k, v, log_a, s0):
    b, t, h, dk = q.shape
    c = GLA_CHUNK if t % GLA_CHUNK == 0 else t
    n = t // c

    def chunks(a):
        return a.astype(jnp.float32).reshape((b, n, c) + a.shape[2:]).swapaxes(0, 1)

    causal = jnp.tril(jnp.ones((c, c), dtype=bool))

    def step(s, inp):
        qi, ki, vi, gi = inp
        cum = jnp.cumsum(gi, axis=1)
        q_dec = qi * jnp.exp(cum)
        att = jnp.einsum('bchd,bshd->bhcs', q_dec, ki * jnp.exp(-cum))
        att = jnp.where(causal, att, 0.0)
        o = jnp.einsum('bhcs,bshv->bchv', att, vi) + jnp.einsum('bchd,bhdv->bchv', q_dec, s)
        last = cum[:, -1]
        s = jnp.exp(last)[..., None] * s + jnp.einsum('bchd,bchv->bhdv', ki * jnp.exp(last[:, None] - cum), vi)
        return s, o

    s, o = lax.scan(step, s0.astype(jnp.float32),
                    (chunks(q * GLA_DK ** -0.5), chunks(k), chunks(v), chunks(log_a)))
    return o.swapaxes(0, 1).reshape(b, t, h, -1), s


def gla_mixer(h, s0, w_in, b_r, w_a1, w_a2, b_a, norm_g, w_o):
    b, t, _ = h.shape
    nk, nv = GLA_HEADS * GLA_DK, GLA_HEADS * GLA_DV
    q, k, v, r = jnp.split(h @ w_in, [nk, 2 * nk, 2 * nk + nv], axis=-1)
    log_a = jax.nn.log_sigmoid(((h @ w_a1) @ w_a2 + b_a).astype(jnp.float32)) / GLA_TAU
    hk, hv = (b, t, GLA_HEADS, GLA_DK), (b, t, GLA_HEADS, GLA_DV)
    o, s_new = gla_scan(q.reshape(hk), k.reshape(hk), v.reshape(hv), log_a.reshape(hk), s0)
    o = rms_norm(o, norm_g) * jax.nn.silu((r + b_r).astype(jnp.float32)).reshape(hv)
    return o.reshape(b, t, nv).astype(h.dtype) @ w_o, s_new.astype(s0.dtype)


def nsa_project(h, w_in, b_gate):
    b, t, _ = h.shape
    nq, nkv = N_HEADS * HEAD_DIM, 6 * NSA_KV_HEADS * HEAD_DIM
    q, kv, g = jnp.split(h @ w_in, [nq, nq + nkv], axis=-1)
    q = q.reshape(b, t, N_HEADS, HEAD_DIM)
    kv = kv.reshape(b, t, 6, NSA_KV_HEADS, HEAD_DIM)
    gates = jax.nn.sigmoid((g + b_gate).astype(jnp.float32)).reshape(b, t, N_HEADS, 3)
    return q, kv, gates


def nsa_compress(kv_c, w_cmp, b_cmp):
    n_chunk = kv_c.shape[0] // NSA_CMP_STRIDE
    parts = NSA_CMP_BLOCK // NSA_CMP_STRIDE
    x = kv_c[: n_chunk * NSA_CMP_STRIDE].reshape(n_chunk, NSA_CMP_STRIDE, 2, NSA_KV_HEADS, HEAD_DIM)
    w = w_cmp.reshape(2, parts, NSA_CMP_STRIDE, HEAD_DIM, HEAD_DIM)
    proj = jnp.einsum('cskgd,kpsde->pckge', x, w)
    n_cmp = n_chunk - parts + 1
    out = b_cmp[None, :, None, :] + proj[0, :n_cmp]
    for p in range(1, parts):
        out = out + proj[p, p:p + n_cmp]
    return out


def cmp_to_sel_map(n_cmp, n_sel):
    c0 = np.arange(n_cmp)[:, None] * NSA_CMP_STRIDE
    s0 = np.arange(n_sel)[None, :] * NSA_SEL_BLOCK
    inter = np.minimum(c0 + NSA_CMP_BLOCK, s0 + NSA_SEL_BLOCK) - np.maximum(c0, s0)
    return (np.maximum(inter, 0) / NSA_CMP_BLOCK).astype(np.float32)


def nsa_seq(q, gates, kv_cs, win_kv, q_pos0, win_pos0, w_cmp, b_cmp, rel_bias):
    tq = q.shape[0]
    seq_len = kv_cs.shape[0]
    G, HPG = NSA_KV_HEADS, N_HEADS // NSA_KV_HEADS
    dt = q.dtype
    cmp = nsa_compress(kv_cs[:, :2], w_cmp, b_cmp).astype(dt)
    n_cmp = cmp.shape[0]
    cmp_end = jnp.arange(n_cmp, dtype=jnp.int32) * NSA_CMP_STRIDE + (NSA_CMP_BLOCK - 1)
    n_sel = -(-seq_len // NSA_SEL_BLOCK)
    sel = jnp.pad(kv_cs[:, 2:], ((0, n_sel * NSA_SEL_BLOCK - seq_len), (0, 0), (0, 0), (0, 0)))
    k_sel, v_sel = sel[:, 0], sel[:, 1]
    imp_map = jnp.asarray(cmp_to_sel_map(n_cmp, n_sel))
    n_top = min(NSA_N_SEL, n_sel)
    qb = NSA_QBLOCK if tq % NSA_QBLOCK == 0 else tq
    n_win = NSA_WINDOW + qb
    win = jnp.pad(win_kv, ((NSA_WINDOW, 0), (0, 0), (0, 0), (0, 0)))
    tbl = rel_bias.reshape(N_BUCKETS, G, HPG)
    gi = jnp.arange(G)[None, :, None]
    scale = HEAD_DIM ** -0.5

    def block(inp):
        qblk, gblk, s = inp
        t0 = q_pos0 + s
        t = t0 + jnp.arange(qb, dtype=jnp.int32)
        qg = (qblk * scale).reshape(qb, G, HPG, HEAD_DIM)
        d_c = t[:, None] - cmp_end[None, :]
        lg = jnp.einsum('qghd,cgd->qghc', qg, cmp[:, 0]).astype(jnp.float32) + shared_key_bias(d_c, rel_bias, G)
        p_c = masked_softmax(lg, (d_c >= 0)[:, None, None, :])
        o_c = jnp.einsum('qghc,cgd->qghd', p_c.astype(dt), cmp[:, 1])
        imp = jnp.einsum('qgc,cs->qgs', p_c.sum(axis=2), imp_map)
        blk = jnp.arange(n_sel, dtype=jnp.int32)[None, :]
        tb = (t // NSA_SEL_BLOCK)[:, None]
        forced = (blk == 0) | (blk == tb) | (blk == tb - 1)
        score = jnp.where((blk <= tb)[:, None, :], jnp.where(forced[:, None, :], NSA_FORCE, imp), NEG_INF)
        top_s, top_i = lax.top_k(score, n_top)
        pos = (top_i[..., None] * NSA_SEL_BLOCK + jnp.arange(NSA_SEL_BLOCK, dtype=jnp.int32)).reshape(qb, G, -1)
        ok = jnp.repeat(top_s > NEG_INF / 2, NSA_SEL_BLOCK, axis=-1)
        k_g, v_g = k_sel[pos, gi], v_sel[pos, gi]
        d_s = t[:, None, None] - pos
        b_s = tbl[rel_bucket(d_s), gi].astype(jnp.float32).transpose(0, 1, 3, 2)
        lg = jnp.einsum('qghd,qgkd->qghk', qg, k_g).astype(jnp.float32) + b_s
        p_s = masked_softmax(lg, (ok & (d_s >= 0))[:, :, None, :])
        o_s = jnp.einsum('qghk,qgkd->qghd', p_s.astype(dt), v_g)
        wk = lax.dynamic_slice_in_dim(win, t0 - win_pos0, n_win, axis=0)
        kpos = t0 - NSA_WINDOW + jnp.arange(n_win, dtype=jnp.int32)
        d_w = t[:, None] - kpos[None, :]
        m_w = (d_w >= 0) & (d_w <= NSA_WINDOW) & (kpos >= win_pos0)[None, :]
        lg = jnp.einsum('qghd,kgd->qghk', qg, wk[:, 0]).astype(jnp.float32) + shared_key_bias(d_w, rel_bias, G)
        p_w = masked_softmax(lg, m_w[:, None, None, :])
        o_w = jnp.einsum('qghk,kgd->qghd', p_w.astype(dt), wk[:, 1])
        gg = gblk.reshape(qb, G, HPG, 3).astype(dt)
        o = gg[..., 0:1] * o_c + gg[..., 1:2] * o_s + gg[..., 2:3] * o_w
        return o.reshape(qb, N_HEADS, HEAD_DIM)

    nb = tq // qb
    starts = jnp.arange(nb, dtype=jnp.int32) * qb
    out = lax.map(block, (q.reshape(nb, qb, N_HEADS, HEAD_DIM), gates.reshape(nb, qb, N_HEADS, 3), starts))
    return out.reshape(tq, N_HEADS, HEAD_DIM)


def nsa_prompt(h, w_in, b_gate, w_cmp, b_cmp, w_o, rel_bias):
    b, t, _ = h.shape
    q, kv, gates = nsa_project(h, w_in, b_gate)

    def one(inp):
        qs, gs, kvs = inp
        return nsa_seq(qs, gs, kvs[:, :4], kvs[:, 4:], 0, 0, w_cmp, b_cmp, rel_bias)

    o = lax.map(one, (q, gates, kv))
    return o.reshape(b, t, -1) @ w_o, kv[:, :, :4], kv[:, t - min(NSA_WINDOW, t):, 4:]


def nsa_sample(h, pool, layer, win_buf, page_table, w_in, b_gate, w_cmp, b_cmp, w_o, rel_bias):
    b, t, _ = h.shape
    past = page_table.shape[1] * PAGE_SIZE
    nbuf = win_buf.shape[1]
    q, kv, gates = nsa_project(h, w_in, b_gate)
    win_all = jnp.concatenate([win_buf.astype(kv.dtype), kv[:, :, 4:]], axis=1)

    def one(inp):
        qs, gs, kvs, pages, ws = inp
        kv_cs = jnp.concatenate([gather_pages(pool, layer, pages).astype(kvs.dtype), kvs[:, :4]], axis=0)
        return nsa_seq(qs, gs, kv_cs, ws, past, past - nbuf, w_cmp, b_cmp, rel_bias)

    o = lax.map(one, (q, gates, kv, page_table, win_all))
    return o.reshape(b, t, -1) @ w_o, kv[:, :, :4], win_all[:, -nbuf:]


def mla_project(h, pos, w_down, q_norm, w_uq, kv_norm, w_uk):
    b, t, _ = h.shape
    cq, ckv, kpe = jnp.split(h @ w_down, [MLA_Q_RANK, MLA_Q_RANK + MLA_KV_RANK], axis=-1)
    q = (rms_norm(cq, q_norm) @ w_uq).reshape(b, t, N_HEADS, MLA_NOPE + MLA_ROPE)
    q_abs = jnp.einsum('bthn,rhn->bthr', q[..., :MLA_NOPE], w_uk)
    q_full = jnp.concatenate([q_abs, apply_rope(q[..., MLA_NOPE:], pos)], axis=-1)
    latent = jnp.concatenate([rms_norm(ckv, kv_norm), apply_rope(kpe, pos)], axis=-1)
    return q_full, latent


def mla_attend(q_full, latent, q_pos, w_uv):
    s = latent.shape[1]
    lg = jnp.einsum('bthc,bsc->bhts', q_full, latent).astype(jnp.float32) * (MLA_NOPE + MLA_ROPE) ** -0.5
    mask = jnp.arange(s, dtype=jnp.int32)[None, :] <= q_pos[:, None]
    p = masked_softmax(lg, mask).astype(latent.dtype)
    o_lat = jnp.einsum('bhts,bsr->bthr', p, latent[..., :MLA_KV_RANK])
    return jnp.einsum('bthr,rhv->bthv', o_lat, w_uv)


def mla_prompt(h, pos, w_down, q_norm, w_uq, kv_norm, w_uk, w_uv, w_o):
    b, t, _ = h.shape
    q_full, latent = mla_project(h, pos, w_down, q_norm, w_uq, kv_norm, w_uk)
    qb = MLA_QBLOCK if t % MLA_QBLOCK == 0 else t
    nb = t // qb
    qs = q_full.reshape(b, nb, qb, N_HEADS, -1).swapaxes(0, 1)
    o = lax.map(lambda inp: mla_attend(inp[0], latent, inp[1], w_uv), (qs, pos.reshape(nb, qb)))
    return o.swapaxes(0, 1).reshape(b, t, -1) @ w_o, latent


def mla_sample(h, pos, pool, layer, page_table, w_down, q_norm, w_uq, kv_norm, w_uk, w_uv, w_o):
    b, t, _ = h.shape
    q_full, latent = mla_project(h, pos, w_down, q_norm, w_uq, kv_norm, w_uk)

    def one(inp):
        qs, lat_new, pages = inp
        lat = jnp.concatenate([gather_pages(pool, layer, pages).astype(lat_new.dtype), lat_new], axis=0)
        return mla_attend(qs[None], lat[None], pos, w_uv)[0]

    o = lax.map(one, (q_full, latent, page_table))
    return o.reshape(b, t, -1) @ w_o, latent


def moba_project(h, w_in):
    b, t, _ = h.shape
    q, kv = jnp.split(h @ w_in, [N_HEADS * HEAD_DIM], axis=-1)
    return q.reshape(b, t, N_HEADS, HEAD_DIM), kv.reshape(b, t, 2, MOBA_KV_HEADS, HEAD_DIM)


def moba_seq(q, kv, q_pos0, rel_bias):
    tq = q.shape[0]
    seq_len = kv.shape[0]
    G, HPG = MOBA_KV_HEADS, N_HEADS // MOBA_KV_HEADS
    dt = q.dtype
    n_full = seq_len // MOBA_BLOCK
    n_top = min(MOBA_TOPK, n_full)
    k_all, v_all = kv[:, 0], kv[:, 1]
    kv_pad = jnp.pad(kv, ((0, MOBA_BLOCK), (0, 0), (0, 0), (0, 0)))
    if n_top > 0:
        k_mean = k_all[: n_full * MOBA_BLOCK].astype(jnp.float32).reshape(n_full, MOBA_BLOCK, G, HEAD_DIM).mean(axis=1)
    tbl = rel_bias.reshape(N_BUCKETS, G, HPG)
    gi = jnp.arange(G)[None, :, None, None]
    hi = jnp.arange(HPG)[None, None, :, None]
    qb = MOBA_QBLOCK if tq % MOBA_QBLOCK == 0 else tq
    scale = HEAD_DIM ** -0.5
    blk_off = jnp.arange(MOBA_BLOCK, dtype=jnp.int32)

    def block(inp):
        qblk, s = inp
        t0 = q_pos0 + s
        t = t0 + jnp.arange(qb, dtype=jnp.int32)
        qg = qblk.reshape(qb, G, HPG, HEAD_DIM)
        own0 = (t0 // MOBA_BLOCK) * MOBA_BLOCK
        own = lax.dynamic_slice_in_dim(kv_pad, own0, MOBA_BLOCK, axis=0)
        d_o = t[:, None] - (own0 + blk_off)[None, :]
        lg_o = jnp.einsum('qghd,kgd->qghk', qg * scale, own[:, 0]).astype(jnp.float32) + shared_key_bias(d_o, rel_bias, G)
        m_o = jnp.broadcast_to((d_o >= 0)[:, None, None, :], lg_o.shape)
        if n_top == 0:
            p = masked_softmax(lg_o, m_o).astype(dt)
            o = jnp.einsum('qghk,kgd->qghd', p, own[:, 1])
        else:
            gate = jnp.einsum('qghd,ngd->qghn', qg.astype(jnp.float32), k_mean)
            past_ok = jnp.arange(n_full, dtype=jnp.int32)[None, :] < (t // MOBA_BLOCK)[:, None]
            gate = jnp.where(past_ok[:, None, None, :], gate, NEG_INF)
            top_s, top_i = lax.top_k(gate, n_top)
            pos = (top_i[..., None] * MOBA_BLOCK + blk_off).reshape(qb, G, HPG, -1)
            ok = jnp.repeat(top_s > NEG_INF / 2, MOBA_BLOCK, axis=-1)
            k_g, v_g = k_all[pos, gi], v_all[pos, gi]
            d_p = t[:, None, None, None] - pos
            lg_p = jnp.einsum('qghd,qghkd->qghk', qg * scale, k_g).astype(jnp.float32) + tbl[rel_bucket(d_p), gi, hi].astype(jnp.float32)
            p = masked_softmax(jnp.concatenate([lg_p, lg_o], axis=-1), jnp.concatenate([ok, m_o], axis=-1)).astype(dt)
            kp = pos.shape[-1]
            o = jnp.einsum('qghk,qghkd->qghd', p[..., :kp], v_g) + jnp.einsum('qghk,kgd->qghd', p[..., kp:], own[:, 1])
        return o.reshape(qb, N_HEADS, HEAD_DIM)

    nb = tq // qb
    starts = jnp.arange(nb, dtype=jnp.int32) * qb
    out = lax.map(block, (q.reshape(nb, qb, N_HEADS, HEAD_DIM), starts))
    return out.reshape(tq, N_HEADS, HEAD_DIM)


def moba_prompt(h, w_in, w_o, rel_bias):
    b, t, _ = h.shape
    q, kv = moba_project(h, w_in)
    o = lax.map(lambda inp: moba_seq(inp[0], inp[1], 0, rel_bias), (q, kv))
    return o.reshape(b, t, -1) @ w_o, kv


def moba_sample(h, pool, layer, page_table, w_in, w_o, rel_bias):
    b, t, _ = h.shape
    past = page_table.shape[1] * PAGE_SIZE
    q, kv = moba_project(h, w_in)

    def one(inp):
        qs, kvs, pages = inp
        full = jnp.concatenate([gather_pages(pool, layer, pages).astype(kvs.dtype), kvs], axis=0)
        return moba_seq(qs, full, past, rel_bias)

    o = lax.map(one, (q, kv, page_table))
    return o.reshape(b, t, -1) @ w_o, kv


def conv_ffn(h, conv_state, w_in, conv_w, conv_b, w_out):
    t = h.shape[1]
    g, u = jnp.split(h @ w_in, 2, axis=-1)
    gp = jnp.concatenate([conv_state.astype(g.dtype), g], axis=1)
    gc = conv_b + gp[:, 0:t] * conv_w[0]
    for w_i in range(1, CONV_W):
        gc = gc + gp[:, w_i:w_i + t] * conv_w[w_i]
    return (jax.nn.silu(gc) * u) @ w_out, gp[:, t:]


def setup_inputs(seed: int = 0) -> dict:
    key = jax.random.key(seed)
    ks = iter(jax.random.split(key, 48))
    f32 = jnp.float32

    def nrm(shape, scale):
        return jax.random.normal(next(ks), shape, f32) * scale

    def gain(shape):
        return 1.0 + nrm(shape, 0.02)

    la, lb, lc, ld = (_layers_of(m) for m in range(N_MIXERS))
    D = D_MODEL
    n_pages = PAST_LEN // PAGE_SIZE
    pool_pages = DEC_BATCH * n_pages * POOL_NUM // POOL_DEN
    win_buf = min(NSA_WINDOW, PAST_LEN)
    out_scale = (2 * DEPTH) ** -0.5
    inputs = {
        'x_prompt': nrm((BATCH, SEQ, D), 1.0),
        'x_sample': nrm((DEC_BATCH, DEC_SEQ, D), 1.0),
        'state_gla': nrm((la, DEC_BATCH, GLA_HEADS, GLA_DK, GLA_DV), 1.0),
        'cache_nsa_kv': nrm((lb, pool_pages, PAGE_SIZE, 4, NSA_KV_HEADS, HEAD_DIM), 1.0),
        'cache_nsa_win': nrm((lb, DEC_BATCH, win_buf, 2, NSA_KV_HEADS, HEAD_DIM), 1.0),
        'cache_mla': nrm((lc, pool_pages, PAGE_SIZE, MLA_KV_RANK + MLA_ROPE), 1.0),
        'cache_moba_kv': nrm((ld, pool_pages, PAGE_SIZE, 2, MOBA_KV_HEADS, HEAD_DIM), 1.0),
        'state_ffn_conv': nrm((DEPTH, DEC_BATCH, CONV_W - 1, D_FF), 1.0),
    }
    perm = jax.random.permutation(next(ks), pool_pages)
    inputs['page_table'] = perm[: DEC_BATCH * n_pages].reshape(DEC_BATCH, n_pages).astype(jnp.int32)
    inputs.update({
        'rel_bias': nrm((N_BUCKETS, N_HEADS), 0.5),
        'norm_mix': gain((DEPTH, D)),
        'norm_ffn': gain((DEPTH, D)),
        'norm_final': gain((D,)),
        'ffn_w_in': nrm((DEPTH, D, 2 * D_FF), D ** -0.5),
        'ffn_conv_w': nrm((DEPTH, CONV_W, D_FF), CONV_W ** -0.5),
        'ffn_conv_b': nrm((DEPTH, D_FF), 0.02),
        'ffn_w_out': nrm((DEPTH, D_FF, D), D_FF ** -0.5 * out_scale),
        'gla_w_in': nrm((la, D, 2 * GLA_HEADS * GLA_DK + 2 * GLA_HEADS * GLA_DV), D ** -0.5),
        'gla_b_r': nrm((la, GLA_HEADS * GLA_DV), 0.02),
        'gla_w_a1': nrm((la, D, GLA_GATE_RANK), D ** -0.5),
        'gla_w_a2': nrm((la, GLA_GATE_RANK, GLA_HEADS * GLA_DK), GLA_GATE_RANK ** -0.5),
        'gla_b_a': nrm((la, GLA_HEADS * GLA_DK), 0.1),
        'gla_norm': gain((la, GLA_DV)),
        'gla_w_o': nrm((la, GLA_HEADS * GLA_DV, D), (GLA_HEADS * GLA_DV) ** -0.5 * out_scale),
        'nsa_w_in': nrm((lb, D, N_HEADS * HEAD_DIM + 6 * NSA_KV_HEADS * HEAD_DIM + 3 * N_HEADS), D ** -0.5),
        'nsa_b_gate': nrm((lb, 3 * N_HEADS), 0.1),
        'nsa_w_cmp': nrm((lb, 2, NSA_CMP_BLOCK * HEAD_DIM, HEAD_DIM), (NSA_CMP_BLOCK * HEAD_DIM) ** -0.5),
        'nsa_b_cmp': nrm((lb, 2, HEAD_DIM), 0.02),
        'nsa_w_o': nrm((lb, N_HEADS * HEAD_DIM, D), (N_HEADS * HEAD_DIM) ** -0.5 * out_scale),
        'mla_w_down': nrm((lc, D, MLA_Q_RANK + MLA_KV_RANK + MLA_ROPE), D ** -0.5),
        'mla_q_norm': gain((lc, MLA_Q_RANK)),
        'mla_w_uq': nrm((lc, MLA_Q_RANK, N_HEADS * (MLA_NOPE + MLA_ROPE)), MLA_Q_RANK ** -0.5),
        'mla_kv_norm': gain((lc, MLA_KV_RANK)),
        'mla_w_uk': nrm((lc, MLA_KV_RANK, N_HEADS, MLA_NOPE), MLA_KV_RANK ** -0.5),
        'mla_w_uv': nrm((lc, MLA_KV_RANK, N_HEADS, MLA_V), MLA_KV_RANK ** -0.5),
        'mla_w_o': nrm((lc, N_HEADS * MLA_V, D), (N_HEADS * MLA_V) ** -0.5 * out_scale),
        'moba_w_in': nrm((ld, D, N_HEADS * HEAD_DIM + 2 * MOBA_KV_HEADS * HEAD_DIM), D ** -0.5),
        'moba_w_o': nrm((ld, N_HEADS * HEAD_DIM, D), (N_HEADS * HEAD_DIM) ** -0.5 * out_scale),
    })
    return inputs


def reference(x_prompt, x_sample, state_gla, cache_nsa_kv, cache_nsa_win, cache_mla, cache_moba_kv, state_ffn_conv,
              page_table, rel_bias, norm_mix, norm_ffn, norm_final, ffn_w_in, ffn_conv_w, ffn_conv_b, ffn_w_out,
              gla_w_in, gla_b_r, gla_w_a1, gla_w_a2, gla_b_a, gla_norm, gla_w_o,
              nsa_w_in, nsa_b_gate, nsa_w_cmp, nsa_b_cmp, nsa_w_o,
              mla_w_down, mla_q_norm, mla_w_uq, mla_kv_norm, mla_w_uk, mla_w_uv, mla_w_o,
              moba_w_in, moba_w_o):
    bp, tp, _ = x_prompt.shape
    ts = x_sample.shape[1]
    past_len = page_table.shape[1] * PAGE_SIZE
    pos_p = jnp.arange(tp, dtype=jnp.int32)
    pos_s = past_len + jnp.arange(ts, dtype=jnp.int32)
    xp, xs = x_prompt, x_sample
    gla_p, gla_s, nkv_p, nkv_s, nwin_p, nwin_s = [], [], [], [], [], []
    mla_p, mla_s, mkv_p, mkv_s, conv_p, conv_s = [], [], [], [], [], []
    for i in range(DEPTH):
        m, j = i % N_MIXERS, i // N_MIXERS
        hp, hs = rms_norm(xp, norm_mix[i]), rms_norm(xs, norm_mix[i])
        if m == 0:
            w = (gla_w_in[j], gla_b_r[j], gla_w_a1[j], gla_w_a2[j], gla_b_a[j], gla_norm[j], gla_w_o[j])
            yp, st = gla_mixer(hp, jnp.zeros((bp, GLA_HEADS, GLA_DK, GLA_DV), hp.dtype), *w)
            gla_p.append(st)
            ys, st = gla_mixer(hs, state_gla[j], *w)
            gla_s.append(st)
        elif m == 1:
            w = (nsa_w_in[j], nsa_b_gate[j], nsa_w_cmp[j], nsa_b_cmp[j], nsa_w_o[j])
            yp, kv, win = nsa_prompt(hp, *w, rel_bias)
            nkv_p.append(kv)
            nwin_p.append(win)
            ys, kv, win = nsa_sample(hs, cache_nsa_kv, j, cache_nsa_win[j], page_table, *w, rel_bias)
            nkv_s.append(kv)
            nwin_s.append(win)
        elif m == 2:
            w = (mla_w_down[j], mla_q_norm[j], mla_w_uq[j], mla_kv_norm[j], mla_w_uk[j], mla_w_uv[j], mla_w_o[j])
            yp, lat = mla_prompt(hp, pos_p, *w)
            mla_p.append(lat)
            ys, lat = mla_sample(hs, pos_s, cache_mla, j, page_table, *w)
            mla_s.append(lat)
        else:
            yp, kv = moba_prompt(hp, moba_w_in[j], moba_w_o[j], rel_bias)
            mkv_p.append(kv)
            ys, kv = moba_sample(hs, cache_moba_kv, j, page_table, moba_w_in[j], moba_w_o[j], rel_bias)
            mkv_s.append(kv)
        xp, xs = xp + yp, xs + ys
        wf = (ffn_w_in[i], ffn_conv_w[i], ffn_conv_b[i], ffn_w_out[i])
        fp, cst = conv_ffn(rms_norm(xp, norm_ffn[i]), jnp.zeros((bp, CONV_W - 1, D_FF), xp.dtype), *wf)
        conv_p.append(cst)
        fs, cst = conv_ffn(rms_norm(xs, norm_ffn[i]), state_ffn_conv[i], *wf)
        conv_s.append(cst)
        xp, xs = xp + fp, xs + fs
    y_prompt = rms_norm(xp, norm_final)
    y_sample = rms_norm(xs, norm_final)
    return (y_prompt, y_sample, jnp.stack(gla_p), jnp.stack(gla_s), jnp.stack(nkv_p), jnp.stack(nkv_s),
            jnp.stack(nwin_p), jnp.stack(nwin_s), jnp.stack(mla_p), jnp.stack(mla_s), jnp.stack(mkv_p),
            jnp.stack(mkv_s), jnp.stack(conv_p), jnp.stack(conv_s))
```

```python
import functools
import math

import numpy as np
import jax
import jax.numpy as jnp
from jax import lax
from jax.experimental import pallas as pl
from jax.experimental.pallas import tpu as pltpu

F32 = jnp.float32
BF16 = jnp.bfloat16

D_MODEL = 2048
DEPTH = 4
PAGE_SIZE = 128
N_MIXERS = 4
N_HEADS = 16
HEAD_DIM = D_MODEL // N_HEADS
N_BUCKETS = 32
MAX_DISTANCE = 128
RMS_EPS = 1e-6
NEG_INF = -1e30

GLA_HEADS = 4
GLA_DK = D_MODEL // 2 // GLA_HEADS
GLA_DV = D_MODEL // GLA_HEADS
GLA_GATE_RANK = 16
GLA_TAU = 16.0
GLA_CHUNK = 64

NSA_KV_HEADS = 2
NSA_CMP_BLOCK = 32
NSA_CMP_STRIDE = 16
NSA_SEL_BLOCK = 64
NSA_N_SEL = 16
NSA_WINDOW = 512
NSA_QBLOCK = 64
NSA_FORCE = 1e9

MLA_Q_RANK = 512
MLA_KV_RANK = 512
MLA_NOPE = 128
MLA_ROPE = 64
MLA_V = 128
ROPE_THETA = 10000.0

MOBA_KV_HEADS = 4
MOBA_BLOCK = 256
MOBA_TOPK = 3

D_FF = 5632
CONV_W = 3

V7X_VMEM_BYTES = 64 * 1024 * 1024
VMEM_LIMIT_BYTES = V7X_VMEM_BYTES * 3 // 4
LANES = 128
SUBLANES_BF16 = 16


def _cparams(*sem):
    return pltpu.CompilerParams(dimension_semantics=sem, vmem_limit_bytes=VMEM_LIMIT_BYTES)


def _pick(n, cap, mult):
    best = None
    for d in range(mult, min(n, cap) + 1, mult):
        if n % d == 0:
            best = d
    assert best is not None, (n, cap, mult)
    return best


def _pad_cols(w, mult=LANES):
    n = w.shape[-1]
    pad = (-n) % mult
    if pad:
        w = jnp.pad(w, [(0, 0)] * (w.ndim - 1) + [(0, pad)])
    return w


def _dot(a, b):
    return jnp.dot(a, b, preferred_element_type=F32)


def _dot_nt(a, b):
    return lax.dot_general(a, b, (((1,), (1,)), ((), ())), preferred_element_type=F32)


def _dot_tn(a, b):
    return lax.dot_general(a, b, (((0,), (0,)), ((), ())), preferred_element_type=F32)


def _sigmoid(x):
    return 1.0 / (1.0 + jnp.exp(-x))


def _mm_kernel(*refs, has_norm, has_res):
    it = iter(refs)
    x_ref, w_ref = next(it), next(it)
    g_ref = next(it) if has_norm else None
    r_ref = next(it) if has_res else None
    o_ref = next(it)
    if has_norm:
        h_ref = next(it)

        @pl.when(pl.program_id(1) == 0)
        def _():
            xf = x_ref[...].astype(F32)
            y = xf * lax.rsqrt(jnp.mean(xf * xf, axis=-1, keepdims=True) + RMS_EPS)
            h_ref[...] = (y * g_ref[...]).astype(BF16)

        lhs = h_ref[...]
    else:
        lhs = x_ref[...].astype(BF16)
    acc = _dot(lhs, w_ref[...])
    if has_res:
        acc = acc + r_ref[...]
    o_ref[...] = acc.astype(o_ref.dtype)


def _mm(x, w, *, x_col=0, norm_g=None, res=None, out_dtype=F32, tm_cap=1024, tn_cap=1024):
    m = x.shape[0]
    k, n = w.shape
    assert n % LANES == 0
    tm = _pick(m, tm_cap, SUBLANES_BF16)
    tn = n if n <= 1536 else _pick(n, tn_cap, LANES)
    if k > 4096:
        tm = _pick(m, 512, SUBLANES_BF16)
    in_specs = [pl.BlockSpec((tm, k), lambda i, j: (i, x_col)),
                pl.BlockSpec((k, tn), lambda i, j: (0, j))]
    args = [x, w]
    scratch = []
    if norm_g is not None:
        in_specs.append(pl.BlockSpec((1, k), lambda i, j: (0, 0)))
        args.append(norm_g.reshape(1, k).astype(F32))
        scratch.append(pltpu.VMEM((tm, k), BF16))
    if res is not None:
        in_specs.append(pl.BlockSpec((tm, tn), lambda i, j: (i, j)))
        args.append(res)
    return pl.pallas_call(
        functools.partial(_mm_kernel, has_norm=norm_g is not None, has_res=res is not None),
        grid=(m // tm, n // tn),
        in_specs=in_specs,
        out_specs=pl.BlockSpec((tm, tn), lambda i, j: (i, j)),
        out_shape=jax.ShapeDtypeStruct((m, n), out_dtype),
        scratch_shapes=scratch,
        compiler_params=_cparams("parallel", "arbitrary"),
    )(*args)


def _rms_kernel(x_ref, g_ref, o_ref):
    xf = x_ref[...]
    y = xf * lax.rsqrt(jnp.mean(xf * xf, axis=-1, keepdims=True) + RMS_EPS)
    o_ref[...] = y * g_ref[...]


def _rms(x, g):
    m, d = x.shape
    tm = _pick(m, 512, 8)
    return pl.pallas_call(
        _rms_kernel,
        grid=(m // tm,),
        in_specs=[pl.BlockSpec((tm, d), lambda i: (i, 0)), pl.BlockSpec((1, d), lambda i: (0, 0))],
        out_specs=pl.BlockSpec((tm, d), lambda i: (i, 0)),
        out_shape=jax.ShapeDtypeStruct((m, d), F32),
        compiler_params=_cparams("parallel"),
    )(x, g.reshape(1, d))


def _gate_kernel(*refs, seq_len, tr, has_state):
    if has_state:
        g_ref, u_ref, gp_ref, e1_ref, e2_ref, cw_ref, cb_ref, o_ref = refs
    else:
        g_ref, u_ref, gp_ref, cw_ref, cb_ref, o_ref = refs
    g = g_ref[...]
    gp = gp_ref[...]
    rl = lax.broadcasted_iota(jnp.int32, (tr, 1), 0)
    g1 = jnp.where(rl == 0, gp[7:8], pltpu.roll(g, 1, axis=0))
    g2 = jnp.where(rl == 0, gp[6:7], jnp.where(rl == 1, gp[7:8], pltpu.roll(g, 2, axis=0)))
    if seq_len >= tr:
        pos = lax.rem(pl.program_id(0) * tr, seq_len) + rl
    else:
        pos = lax.rem(rl, seq_len)
    if has_state:
        e1, e2 = e1_ref[...], e2_ref[...]
    else:
        e1 = e2 = jnp.zeros_like(g)
    g1 = jnp.where(pos >= 1, g1, e1)
    g2 = jnp.where(pos >= 2, g2, e2)
    cw = cw_ref[...]
    gc = cb_ref[...] + g2 * cw[0:1] + g1 * cw[1:2] + g * cw[2:3]
    o_ref[...] = (gc * _sigmoid(gc) * u_ref[...]).astype(o_ref.dtype)


def _ffn_gate(gu, conv_w, conv_b, seq_len, state):
    m = gu.shape[0]
    tc = _pick(D_FF, 1024, LANES)
    ncb = D_FF // tc
    tr = _pick(seq_len, 512, 8) if seq_len >= 16 else _pick(m, 512, 8 * seq_len // math.gcd(8, seq_len))
    if seq_len < tr:
        assert tr % seq_len == 0 and seq_len >= CONV_W - 1
    else:
        assert seq_len % tr == 0
    in_specs = [pl.BlockSpec((tr, tc), lambda i, j: (i, j)),
                pl.BlockSpec((tr, tc), lambda i, j: (i, j + ncb)),
                pl.BlockSpec((8, tc), lambda i, j: (jnp.maximum(i * (tr // 8) - 1, 0), j))]
    args = [gu, gu, gu]
    if state is not None:
        b = state.shape[0]
        z = jnp.zeros((b, seq_len, D_FF), F32)
        e1 = z.at[:, 0].set(state[:, 1]).reshape(m, D_FF)
        e2 = z.at[:, 0].set(state[:, 0]).at[:, 1].set(state[:, 1]).reshape(m, D_FF)
        in_specs += [pl.BlockSpec((tr, tc), lambda i, j: (i, j))] * 2
        args += [e1, e2]
    in_specs += [pl.BlockSpec((8, tc), lambda i, j: (0, j)), pl.BlockSpec((1, tc), lambda i, j: (0, j))]
    args += [jnp.pad(conv_w, ((0, 8 - CONV_W), (0, 0))), conv_b.reshape(1, D_FF)]
    return pl.pallas_call(
        functools.partial(_gate_kernel, seq_len=seq_len, tr=tr, has_state=state is not None),
        grid=(m // tr, ncb),
        in_specs=in_specs,
        out_specs=pl.BlockSpec((tr, tc), lambda i, j: (i, j)),
        out_shape=jax.ShapeDtypeStruct((m, D_FF), BF16),
        compiler_params=_cparams("parallel", "parallel"),
    )(*args)


def _cumsum_rows(x):
    c = x.shape[0]
    row = lax.broadcasted_iota(jnp.int32, x.shape, 0)
    s = 1
    while s < c:
        x = x + jnp.where(row >= s, pltpu.roll(x, s, axis=0), 0.0)
        s *= 2
    return x


def _gla_kernel(*refs, c, has_state):
    it = iter(refs)
    q_ref, k_ref, v_ref, r_ref, ga_ref, wa2_ref, ba_ref, ng_ref, br_ref = (next(it) for _ in range(9))
    s0_ref = next(it) if has_state else None
    o_ref, sout_ref, st_ref = next(it), next(it), next(it)
    ci = pl.program_id(2)

    @pl.when(ci == 0)
    def _():
        if has_state:
            st_ref[...] = s0_ref[...].T
        else:
            st_ref[...] = jnp.zeros_like(st_ref)

    q = q_ref[...] * (GLA_DK ** -0.5)
    k = k_ref[...]
    v16 = v_ref[...].astype(BF16)
    lr = _dot(ga_ref[...].astype(BF16), wa2_ref[...]) + ba_ref[...]
    log_a = (jnp.minimum(lr, 0.0) - jnp.log(1.0 + jnp.exp(-jnp.abs(lr)))) / GLA_TAU
    cum = _cumsum_rows(log_a)
    q_dec = (q * jnp.exp(cum)).astype(BF16)
    k_inv = (k * jnp.exp(-cum)).astype(BF16)
    att = _dot_nt(q_dec, k_inv)
    ri = lax.broadcasted_iota(jnp.int32, (c, c), 0)
    cj = lax.broadcasted_iota(jnp.int32, (c, c), 1)
    att = jnp.where(ri >= cj, att, 0.0)
    st = st_ref[...]
    o = _dot(att.astype(BF16), v16) + _dot_nt(q_dec, st.astype(BF16))
    last = cum[c - 1:c]
    k_dec = (k * jnp.exp(last - cum)).astype(BF16)
    st_new = st * jnp.exp(last) + _dot_tn(v16, k_dec)
    st_ref[...] = st_new
    on = o * lax.rsqrt(jnp.mean(o * o, axis=-1, keepdims=True) + RMS_EPS) * ng_ref[...]
    rr = r_ref[...] + br_ref[...]
    o_ref[...] = (on * (rr * _sigmoid(rr))).astype(o_ref.dtype)

    @pl.when(ci == pl.num_programs(2) - 1)
    def _():
        sout_ref[...] = st_new.T


def _gla_scan(p, b, t, w_a2p, b_a, norm_g, b_r, s0):
    c = GLA_CHUNK if t % GLA_CHUNK == 0 else t
    nch = t // c
    h = GLA_HEADS
    nk = h * GLA_DK
    kq, kv_ = GLA_DK, GLA_DV
    row = lambda bi, hi, ci: bi * nch + ci
    in_specs = [
        pl.BlockSpec((c, kq), lambda bi, hi, ci: (row(bi, hi, ci), hi)),
        pl.BlockSpec((c, kq), lambda bi, hi, ci: (row(bi, hi, ci), h + hi)),
        pl.BlockSpec((c, kv_), lambda bi, hi, ci: (row(bi, hi, ci), 2 * nk // kv_ + hi)),
        pl.BlockSpec((c, kv_), lambda bi, hi, ci: (row(bi, hi, ci), 2 * nk // kv_ + h + hi)),
        pl.BlockSpec((c, LANES), lambda bi, hi, ci: (row(bi, hi, ci), (2 * nk + 2 * h * kv_) // LANES)),
        pl.BlockSpec((LANES, kq), lambda bi, hi, ci: (0, hi)),
        pl.BlockSpec((1, kq), lambda bi, hi, ci: (0, hi)),
        pl.BlockSpec((1, kv_), lambda bi, hi, ci: (0, 0)),
        pl.BlockSpec((1, kv_), lambda bi, hi, ci: (0, hi)),
    ]
    args = [p, p, p, p, p, w_a2p, b_a.reshape(1, nk), norm_g.reshape(1, kv_), b_r.reshape(1, h * kv_)]
    if s0 is not None:
        in_specs.append(pl.BlockSpec((None, None, kq, kv_), lambda bi, hi, ci: (bi, hi, 0, 0)))
        args.append(s0)
    return pl.pallas_call(
        functools.partial(_gla_kernel, c=c, has_state=s0 is not None),
        grid=(b, h, nch),
        in_specs=in_specs,
        out_specs=[pl.BlockSpec((c, kv_), lambda bi, hi, ci: (row(bi, hi, ci), hi)),
                   pl.BlockSpec((None, None, kq, kv_), lambda bi, hi, ci: (bi, hi, 0, 0))],
        out_shape=[jax.ShapeDtypeStruct((b * t, h * kv_), BF16),
                   jax.ShapeDtypeStruct((b, h, kq, kv_), F32)],
        scratch_shapes=[pltpu.VMEM((kv_, kq), F32)],
        compiler_params=_cparams("parallel", "parallel", "arbitrary"),
    )(*args)


def _gla_layer(x, b, t, s0, norm_mix, w_in, b_r, w_a1, w_a2, b_a, norm_g, w_o):
    w_cat = jnp.concatenate([w_in, _pad_cols(w_a1)], axis=1).astype(BF16)
    p = _mm(x, w_cat, norm_g=norm_mix)
    w_a2p = jnp.pad(w_a2, ((0, LANES - GLA_GATE_RANK), (0, 0))).astype(BF16)
    o, s_new = _gla_scan(p, b, t, w_a2p, b_a, norm_g, b_r, s0)
    return _mm(o, w_o.astype(BF16), res=x), s_new


def _rel_bucket_np(dist):
    n = np.maximum(dist, 0)
    exact = N_BUCKETS // 2
    lg = np.log(np.maximum(n, 1).astype(np.float32) / np.float32(exact)) / np.float32(math.log(MAX_DISTANCE / exact))
    large = np.minimum(exact + (lg * np.float32(N_BUCKETS - exact)).astype(np.int32), N_BUCKETS - 1)
    return np.where(n < exact, n, large)


def rel_bucket(dist):
    n = jnp.maximum(dist, 0)
    exact = N_BUCKETS // 2
    lg = jnp.log(jnp.maximum(n, 1).astype(jnp.float32) / exact) / math.log(MAX_DISTANCE / exact)
    large = jnp.minimum(exact + (lg * (N_BUCKETS - exact)).astype(jnp.int32), N_BUCKETS - 1)
    return jnp.where(n < exact, n, large)


def _toeplitz_bias(rel_bias, tq, tk, c):
    d = c + jnp.arange(tq, dtype=jnp.int32)[:, None] - jnp.arange(tk, dtype=jnp.int32)[None, :]
    b = rel_bias[rel_bucket(d)].astype(F32)
    return b.transpose(2, 0, 1).reshape(N_HEADS * tq, tk)


def _rope_tables(pos):
    half = MLA_ROPE // 2
    inv = ROPE_THETA ** (-jnp.arange(half, dtype=jnp.float32) / half)
    ang = pos.astype(jnp.float32)[:, None] * inv
    cos, sin = jnp.cos(ang), jnp.sin(ang)
    z = jnp.zeros_like(cos)
    zz = jnp.zeros((pos.shape[0], LANES - MLA_ROPE), F32)
    c = jnp.concatenate([cos, cos, zz], axis=1)
    sa = jnp.concatenate([-sin, z, zz], axis=1)
    sb = jnp.concatenate([z, sin, zz], axis=1)
    return c, sa, sb


def _rope128(x, c, sa, sb):
    return x * c + pltpu.roll(x, LANES - MLA_ROPE // 2, axis=1) * sa + pltpu.roll(x, MLA_ROPE // 2, axis=1) * sb


def _mla_latent_kernel(ckv_ref, kpe_ref, g_ref, c_ref, sa_ref, sb_ref, o_ref):
    x = ckv_ref[...]
    y = x * lax.rsqrt(jnp.mean(x * x, axis=-1, keepdims=True) + RMS_EPS) * g_ref[...]
    kp = _rope128(kpe_ref[...], c_ref[...], sa_ref[...], sb_ref[...])
    o_ref[:, 0:MLA_KV_RANK] = y
    o_ref[:, MLA_KV_RANK:MLA_KV_RANK + MLA_ROPE] = kp[:, 0:MLA_ROPE]


def _mla_q_kernel(xn_ref, xp_ref, w_ref, c_ref, sa_ref, sb_ref, qa_ref, qp_ref):
    qa_ref[...] = _dot(xn_ref[...].astype(BF16), w_ref[...]).astype(qa_ref.dtype)
    qp_ref[...] = _rope128(xp_ref[...], c_ref[...], sa_ref[...], sb_ref[...]).astype(qp_ref.dtype)


def _mla_project(x, t, pos, norm_mix, w_down, q_norm, w_uq, kv_norm, w_uk):
    m = x.shape[0]
    h = N_HEADS
    d = _mm(x, _pad_cols(w_down).astype(BF16), norm_g=norm_mix)
    w3 = w_uq.reshape(MLA_Q_RANK, h, MLA_NOPE + MLA_ROPE)
    w_n = w3[:, :, :MLA_NOPE].reshape(MLA_Q_RANK, h * MLA_NOPE)
    w_p = jnp.pad(w3[:, :, MLA_NOPE:], ((0, 0), (0, 0), (0, LANES - MLA_ROPE))).reshape(MLA_Q_RANK, h * LANES)
    q = _mm(d, jnp.concatenate([w_n, w_p], axis=1).astype(BF16), norm_g=q_norm)
    tabs = _rope_tables(pos)
    period = pos.shape[0]
    tm = _pick(m, 512, SUBLANES_BF16)
    if period < tm:
        assert tm % period == 0
        tabs = tuple(jnp.tile(tb, (tm // period, 1)) for tb in tabs)
        ntab = 1
    else:
        assert period % tm == 0
        ntab = period // tm
    latent = pl.pallas_call(
        _mla_latent_kernel,
        grid=(m // tm,),
        in_specs=[pl.BlockSpec((tm, MLA_KV_RANK), lambda i: (i, 1)),
                  pl.BlockSpec((tm, LANES), lambda i: (i, (MLA_Q_RANK + MLA_KV_RANK) // LANES)),
                  pl.BlockSpec((1, MLA_KV_RANK), lambda i: (0, 0))]
        + [pl.BlockSpec((tm, LANES), lambda i: (i % ntab, 0))] * 3,
        out_specs=pl.BlockSpec((tm, MLA_KV_RANK + MLA_ROPE), lambda i: (i, 0)),
        out_shape=jax.ShapeDtypeStruct((m, MLA_KV_RANK + MLA_ROPE), F32),
        compiler_params=_cparams("parallel"),
    )(d, d, kv_norm.reshape(1, MLA_KV_RANK), *tabs)
    w_ukt = w_uk.transpose(1, 2, 0).astype(BF16)
    qa, qp = pl.pallas_call(
        _mla_q_kernel,
        grid=(m // tm, h),
        in_specs=[pl.BlockSpec((tm, MLA_NOPE), lambda i, hi: (i, hi)),
                  pl.BlockSpec((tm, LANES), lambda i, hi: (i, h + hi)),
                  pl.BlockSpec((None, MLA_NOPE, MLA_KV_RANK), lambda i, hi: (hi, 0, 0))]
        + [pl.BlockSpec((tm, LANES), lambda i, hi: (i % ntab, 0))] * 3,
        out_specs=[pl.BlockSpec((None, tm, MLA_KV_RANK), lambda i, hi: (hi, i, 0)),
                   pl.BlockSpec((None, tm, LANES), lambda i, hi: (hi, i, 0))],
        out_shape=[jax.ShapeDtypeStruct((h, m, MLA_KV_RANK), BF16),
                   jax.ShapeDtypeStruct((h, m, LANES), BF16)],
        compiler_params=_cparams("parallel", "parallel"),
    )(q, q, w_ukt, *tabs)
    return qa, qp, latent


def _softmax_step(s, mask, m_ref, l_ref):
    z = jnp.where(mask, s, NEG_INF) if mask is not None else s
    m_old = m_ref[...]
    m_new = jnp.maximum(m_old, jnp.max(z, axis=-1, keepdims=True))
    p = jnp.exp(z - m_new)
    if mask is not None:
        p = jnp.where(mask, p, 0.0)
    alpha = jnp.exp(m_old - m_new)
    l_ref[...] = alpha * l_ref[...] + jnp.sum(p, axis=-1, keepdims=True)
    m_ref[...] = m_new
    return alpha, p


def _mla_finish(acc_ref, l_ref, wuv_ref, o_ref, tq):
    o_lat = (acc_ref[...] / jnp.maximum(l_ref[...], 1e-30)).astype(BF16)
    for hi in range(N_HEADS):
        o_ref[:, hi * MLA_V:(hi + 1) * MLA_V] = _dot(o_lat[hi * tq:(hi + 1) * tq], wuv_ref[hi]).astype(o_ref.dtype)


MLA_SCALE = (MLA_NOPE + MLA_ROPE) ** -0.5


def _mla_prompt_kernel(qa_ref, qp_ref, lat_ref, wuv_ref, o_ref, m_ref, l_ref, acc_ref, *, tq, tk):
    qt, kt = pl.program_id(1), pl.program_id(2)
    rows = N_HEADS * tq

    @pl.when(kt == 0)
    def _():
        m_ref[...] = jnp.full_like(m_ref, NEG_INF)
        l_ref[...] = jnp.zeros_like(l_ref)
        acc_ref[...] = jnp.zeros_like(acc_ref)

    @pl.when(kt * tk <= qt * tq + tq - 1)
    def _():
        kc = lat_ref[:, 0:MLA_KV_RANK].astype(BF16)
        kp = lat_ref[:, MLA_KV_RANK:MLA_KV_RANK + MLA_ROPE].astype(BF16)
        qa = qa_ref[...].reshape(rows, MLA_KV_RANK)
        qp = qp_ref[...].reshape(rows, LANES)[:, 0:MLA_ROPE]
        s = (_dot_nt(qa, kc) + _dot_nt(qp, kp)) * MLA_SCALE
        ti = qt * tq + lax.rem(lax.broadcasted_iota(jnp.int32, (rows, tk), 0), tq)
        kj = kt * tk + lax.broadcasted_iota(jnp.int32, (rows, tk), 1)
        alpha, p = _softmax_step(s, kj <= ti, m_ref, l_ref)
        acc_ref[...] = alpha * acc_ref[...] + _dot(p.astype(BF16), kc)

    @pl.when(kt == pl.num_programs(2) - 1)
    def _():
        _mla_finish(acc_ref, l_ref, wuv_ref, o_ref, tq)


def _mla_prompt_attend(qa, qp, latent, b, t, w_uv16):
    h = N_HEADS
    tq = _pick(t, 128, SUBLANES_BF16)
    tk = _pick(t, 512, 8)
    nq, nk = t // tq, t // tk
    rows = h * tq

    def kmap(bi, qt, kt):
        return (bi * nk + jnp.minimum(kt, (qt * tq + tq - 1) // tk), 0)

    return pl.pallas_call(
        functools.partial(_mla_prompt_kernel, tq=tq, tk=tk),
        grid=(b, nq, nk),
        in_specs=[pl.BlockSpec((h, tq, MLA_KV_RANK), lambda bi, qt, kt: (0, bi * nq + qt, 0)),
                  pl.BlockSpec((h, tq, LANES), lambda bi, qt, kt: (0, bi * nq + qt, 0)),
                  pl.BlockSpec((tk, MLA_KV_RANK + MLA_ROPE), kmap),
                  pl.BlockSpec((h, MLA_KV_RANK, MLA_V), lambda bi, qt, kt: (0, 0, 0))],
        out_specs=pl.BlockSpec((tq, h * MLA_V), lambda bi, qt, kt: (bi * nq + qt, 0)),
        out_shape=jax.ShapeDtypeStruct((b * t, h * MLA_V), BF16),
        scratch_shapes=[pltpu.VMEM((rows, 1), F32), pltpu.VMEM((rows, 1), F32), pltpu.VMEM((rows, MLA_KV_RANK), F32)],
        compiler_params=_cparams("parallel", "parallel", "arbitrary"),
    )(qa, qp, latent, w_uv16)


def _mla_sample_kernel(pt_ref, qa_ref, qp_ref, *rest, npp, ts):
    page_refs = rest[:npp]
    new_ref, wuv_ref, o_ref, m_ref, l_ref, acc_ref = rest[npp:]
    st = pl.program_id(1)
    rows = N_HEADS * ts

    @pl.when(st == 0)
    def _():
        m_ref[...] = jnp.full_like(m_ref, NEG_INF)
        l_ref[...] = jnp.zeros_like(l_ref)
        acc_ref[...] = jnp.zeros_like(acc_ref)

    qa = qa_ref[...]
    qp = qp_ref[:, 0:MLA_ROPE]
    kcs = [r[:, 0:MLA_KV_RANK].astype(BF16) for r in page_refs]
    kps = [r[:, MLA_KV_RANK:MLA_KV_RANK + MLA_ROPE].astype(BF16) for r in page_refs]
    s = jnp.concatenate([_dot_nt(qa, kc) + _dot_nt(qp, kp) for kc, kp in zip(kcs, kps)], axis=1) * MLA_SCALE
    alpha, p = _softmax_step(s, None, m_ref, l_ref)
    p16 = p.astype(BF16)
    pv = _dot(p16[:, 0:PAGE_SIZE], kcs[0])
    for j in range(1, npp):
        pv = pv + _dot(p16[:, j * PAGE_SIZE:(j + 1) * PAGE_SIZE], kcs[j])
    acc_ref[...] = alpha * acc_ref[...] + pv

    @pl.when(st == pl.num_programs(1) - 1)
    def _():
        kc = new_ref[:, 0:MLA_KV_RANK].astype(BF16)
        kp = new_ref[:, MLA_KV_RANK:MLA_KV_RANK + MLA_ROPE].astype(BF16)
        s2 = (_dot_nt(qa, kc) + _dot_nt(qp, kp)) * MLA_SCALE
        ti = lax.rem(lax.broadcasted_iota(jnp.int32, (rows, ts), 0), ts)
        kj = lax.broadcasted_iota(jnp.int32, (rows, ts), 1)
        alpha2, p2 = _softmax_step(s2, kj <= ti, m_ref, l_ref)
        acc_ref[...] = alpha2 * acc_ref[...] + _dot(p2.astype(BF16), kc)
        _mla_finish(acc_ref, l_ref, wuv_ref, o_ref, ts)


def _mla_sample_attend(qa, qp, latent, cache, layer, page_table, b, ts, w_uv16):
    h = N_HEADS
    npg = page_table.shape[1]
    npp = _pick(npg, 8, 1)
    nst = npg // npp
    rows = h * ts
    qa_s = qa.reshape(h, b, ts, MLA_KV_RANK).transpose(1, 0, 2, 3).reshape(b, rows, MLA_KV_RANK)
    qp_s = qp.reshape(h, b, ts, LANES).transpose(1, 0, 2, 3).reshape(b, rows, LANES)
    lat3 = latent.reshape(b, ts, MLA_KV_RANK + MLA_ROPE)

    def page_map(j):
        return lambda bi, st, pt: (layer, pt[bi * npg + st * npp + j], 0, 0)

    grid_spec = pltpu.PrefetchScalarGridSpec(
        num_scalar_prefetch=1,
        grid=(b, nst),
        in_specs=[pl.BlockSpec((None, rows, MLA_KV_RANK), lambda bi, st, pt: (bi, 0, 0)),
                  pl.BlockSpec((None, rows, LANES), lambda bi, st, pt: (bi, 0, 0))]
        + [pl.BlockSpec((None, None, PAGE_SIZE, MLA_KV_RANK + MLA_ROPE), page_map(j)) for j in range(npp)]
        + [pl.BlockSpec((None, ts, MLA_KV_RANK + MLA_ROPE), lambda bi, st, pt: (bi, 0, 0)),
           pl.BlockSpec((h, MLA_KV_RANK, MLA_V), lambda bi, st, pt: (0, 0, 0))],
        out_specs=pl.BlockSpec((None, ts, h * MLA_V), lambda bi, st, pt: (bi, 0, 0)),
        scratch_shapes=[pltpu.VMEM((rows, 1), F32), pltpu.VMEM((rows, 1), F32), pltpu.VMEM((rows, MLA_KV_RANK), F32)],
    )
    out = pl.pallas_call(
        functools.partial(_mla_sample_kernel, npp=npp, ts=ts),
        grid_spec=grid_spec,
        out_shape=jax.ShapeDtypeStruct((b, ts, h * MLA_V), BF16),
        compiler_params=_cparams("parallel", "arbitrary"),
    )(page_table.reshape(-1), qa_s, qp_s, *([cache] * npp), lat3, w_uv16)
    return out.reshape(b * ts, h * MLA_V)


def _mla_layer(x, b, t, pos, cache, layer, page_table, norm_mix, w_down, q_norm, w_uq, kv_norm, w_uk, w_uv, w_o):
    qa, qp, latent = _mla_project(x, t, pos, norm_mix, w_down, q_norm, w_uq, kv_norm, w_uk)
    w_uv16 = w_uv.transpose(1, 0, 2).astype(BF16)
    if cache is None:
        o = _mla_prompt_attend(qa, qp, latent, b, t, w_uv16)
    else:
        o = _mla_sample_attend(qa, qp, latent, cache, layer, page_table, b, t, w_uv16)
    return _mm(o, w_o.astype(BF16), res=x), latent.reshape(b, t, MLA_KV_RANK + MLA_ROPE)


MOBA_SCALE = HEAD_DIM ** -0.5
MOBA_HPG = N_HEADS // MOBA_KV_HEADS


def _group_rows(q_ref, g, hpg):
    return jnp.concatenate([q_ref[:, (g * hpg + j) * HEAD_DIM:(g * hpg + j + 1) * HEAD_DIM] for j in range(hpg)],
                           axis=0)


def _lane_put(ref, rows, n, col):
    cur = ref[rows, :]
    lane = lax.broadcasted_iota(jnp.int32, cur.shape, 1)
    ref[rows, :] = jnp.where(lane == n, col, cur)


def _moba_kernel(pt_ref, q_ref, *rest, tq, pos0, n_past, paged, own_keys):
    nkv = 2
    kv_refs = rest[:nkv]
    own_refs = rest[nkv:nkv + 2]
    bp_ref, bo_ref, o_ref, m_all, l_all, g_all, o_all = rest[nkv + 2:]
    qt, n = pl.program_id(1), pl.program_id(2)
    hpg, grp = MOBA_HPG, MOBA_KV_HEADS
    gw = grp * HEAD_DIM
    rg = hpg * tq
    t0 = pos0 + qt * tq
    own = t0 // MOBA_BLOCK

    @pl.when(n == 0)
    def _():
        m_all[...] = jnp.full_like(m_all, NEG_INF)
        g_all[...] = jnp.full_like(g_all, NEG_INF)
        l_all[...] = jnp.zeros_like(l_all)
        o_all[...] = jnp.zeros_like(o_all)

    @pl.when(jnp.logical_and(n < n_past, n < own))
    def _():
        if paged:
            kblk = jnp.concatenate([r[:, 0:gw] for r in kv_refs], axis=0)
            vblk = jnp.concatenate([r[:, gw:2 * gw] for r in kv_refs], axis=0)
        else:
            kblk, vblk = kv_refs[0][...], kv_refs[1][...]
        for g in range(grp):
            rows = slice(g * rg, (g + 1) * rg)
            qg = _group_rows(q_ref, g, hpg)
            kg = kblk[:, g * HEAD_DIM:(g + 1) * HEAD_DIM].astype(BF16)
            vg = vblk[:, g * HEAD_DIM:(g + 1) * HEAD_DIM].astype(BF16)
            gate = jnp.sum(_dot_nt(qg.astype(BF16), kg), axis=-1, keepdims=True) * (1.0 / MOBA_BLOCK)
            s = _dot_nt((qg * MOBA_SCALE).astype(BF16), kg) + bp_ref[rows, :]
            mx = jnp.max(s, axis=-1, keepdims=True)
            p = jnp.exp(s - mx)
            _lane_put(m_all, rows, n, mx)
            _lane_put(l_all, rows, n, jnp.sum(p, axis=-1, keepdims=True))
            _lane_put(g_all, rows, n, gate)
            o_all[n, rows, :] = _dot(p.astype(BF16), vg)

    @pl.when(n == pl.num_programs(2) - 1)
    def _():
        ko, vo = own_refs[0][...], own_refs[1][...]
        nko = bo_ref.shape[1]
        if own_keys < nko:
            zpad = jnp.zeros((nko - own_keys, gw), F32)
            ko = jnp.concatenate([ko, zpad], axis=0)
            vo = jnp.concatenate([vo, zpad], axis=0)
        ti = t0 + lax.rem(lax.broadcasted_iota(jnp.int32, (rg, nko), 0), tq)
        kj = own * MOBA_BLOCK + lax.broadcasted_iota(jnp.int32, (rg, nko), 1)
        mask = kj <= ti
        lane = lax.broadcasted_iota(jnp.int32, (rg, LANES), 1)
        for g in range(grp):
            rows = slice(g * rg, (g + 1) * rg)
            qg = _group_rows(q_ref, g, hpg)
            kg = ko[:, g * HEAD_DIM:(g + 1) * HEAD_DIM].astype(BF16)
            vg = vo[:, g * HEAD_DIM:(g + 1) * HEAD_DIM].astype(BF16)
            s = _dot_nt((qg * MOBA_SCALE).astype(BF16), kg) + bo_ref[rows, :]
            z = jnp.where(mask, s, NEG_INF)
            m_o = jnp.max(z, axis=-1, keepdims=True)
            p = jnp.where(mask, jnp.exp(z - m_o), 0.0)
            l_o = jnp.sum(p, axis=-1, keepdims=True)
            o_o = _dot(p.astype(BF16), vg)
            gg = g_all[rows, :]
            sel = jnp.zeros((rg, LANES), jnp.bool_)
            for _ in range(MOBA_TOPK):
                mx = jnp.max(gg, axis=-1, keepdims=True)
                idx = jnp.min(jnp.where(gg == mx, lane, LANES), axis=-1, keepdims=True)
                hit = lane == idx
                sel = jnp.logical_or(sel, jnp.logical_and(hit, mx > NEG_INF / 2))
                gg = jnp.where(hit, -3e38, gg)
            mm = m_all[rows, :]
            m_tot = jnp.maximum(m_o, jnp.max(jnp.where(sel, mm, NEG_INF), axis=-1, keepdims=True))
            w = jnp.where(sel, jnp.exp(mm - m_tot), 0.0)
            w_o = jnp.exp(m_o - m_tot)
            l_tot = jnp.sum(w * l_all[rows, :], axis=-1, keepdims=True) + w_o * l_o
            o = w_o * o_o
            for nb in range(n_past):
                o = o + w[:, nb:nb + 1] * o_all[nb, rows, :]
            o = o / jnp.maximum(l_tot, 1e-30)
            for j in range(hpg):
                hh = g * hpg + j
                o_ref[:, hh * HEAD_DIM:(hh + 1) * HEAD_DIM] = o[j * tq:(j + 1) * tq].astype(o_ref.dtype)


def _moba_attend(p, b, t, pos0, rel_bias, cache, layer, page_table):
    h, grp = N_HEADS, MOBA_KV_HEADS
    gw = grp * HEAD_DIM
    paged = cache is not None
    assert pos0 % MOBA_BLOCK == 0
    if paged:
        assert t <= MOBA_BLOCK
        tq, nq = t, 1
        n_past = pos0 // MOBA_BLOCK
        own_keys = t
        nko = LANES
        ppb = MOBA_BLOCK // PAGE_SIZE
        npg = page_table.shape[1]
        pt = page_table.reshape(-1)
        cache4 = cache.reshape(cache.shape[0], cache.shape[1], PAGE_SIZE, 2 * gw)

        def page_map(j):
            return lambda bi, qt, n, pt_: (layer, pt_[bi * npg + jnp.minimum(n, n_past - 1) * ppb + j], 0, 0)

        kv_specs = [pl.BlockSpec((None, None, PAGE_SIZE, 2 * gw), page_map(j)) for j in range(ppb)]
        kv_args = [cache4] * ppb
        own_specs = [pl.BlockSpec((t, gw), lambda bi, qt, n, pt_: (bi, h * HEAD_DIM // gw)),
                     pl.BlockSpec((t, gw), lambda bi, qt, n, pt_: (bi, h * HEAD_DIM // gw + 1))]
        bias_past = jnp.stack([jnp.broadcast_to(jnp.repeat(rel_bias[N_BUCKETS - 1], tq)[:, None], (h * tq, MOBA_BLOCK)),
                               _toeplitz_bias(rel_bias, tq, MOBA_BLOCK, MOBA_BLOCK)])
        bias_own = _toeplitz_bias(rel_bias, tq, nko, 0)[None]

        def bp_map(bi, qt, n, pt_):
            return (jnp.where(n >= n_past - 1, 1, 0), 0, 0)

        def bo_map(bi, qt, n, pt_):
            return (0, 0, 0)
    else:
        assert t % MOBA_BLOCK == 0
        tq = 128
        nq = t // tq
        nblk = t // MOBA_BLOCK
        n_past = nblk - 1
        own_keys = MOBA_BLOCK
        nko = MOBA_BLOCK
        pt = jnp.zeros((1,), jnp.int32)
        qpb = MOBA_BLOCK // tq

        def past_map(col):
            return lambda bi, qt, n, pt_: (bi * nblk + jnp.minimum(n, jnp.maximum(qt // qpb - 1, 0)), col)

        kcol = h * HEAD_DIM // gw
        kv_specs = [pl.BlockSpec((MOBA_BLOCK, gw), past_map(kcol)), pl.BlockSpec((MOBA_BLOCK, gw), past_map(kcol + 1))]
        kv_args = [p, p]
        own_specs = [pl.BlockSpec((MOBA_BLOCK, gw), lambda bi, qt, n, pt_: (bi * nblk + qt // qpb, kcol)),
                     pl.BlockSpec((MOBA_BLOCK, gw), lambda bi, qt, n, pt_: (bi * nblk + qt // qpb, kcol + 1))]
        far = jnp.broadcast_to(jnp.repeat(rel_bias[N_BUCKETS - 1], tq)[:, None], (h * tq, MOBA_BLOCK))
        bias_past = jnp.stack([far] + [_toeplitz_bias(rel_bias, tq, MOBA_BLOCK, MOBA_BLOCK + r * tq) for r in range(qpb)])
        bias_own = jnp.stack([_toeplitz_bias(rel_bias, tq, MOBA_BLOCK, r * tq) for r in range(qpb)])

        def bp_map(bi, qt, n, pt_):
            return (jnp.where(n == qt // qpb - 1, 1 + qt % qpb, 0), 0, 0)

        def bo_map(bi, qt, n, pt_):
            return (qt % qpb, 0, 0)

    rows = h * tq
    grid_spec = pltpu.PrefetchScalarGridSpec(
        num_scalar_prefetch=1,
        grid=(b, nq, n_past + 1),
        in_specs=[pl.BlockSpec((tq, h * HEAD_DIM), lambda bi, qt, n, pt_: (bi * nq + qt, 0))]
        + kv_specs + own_specs
        + [pl.BlockSpec((None, rows, MOBA_BLOCK), bp_map), pl.BlockSpec((None, rows, nko), bo_map)],
        out_specs=pl.BlockSpec((tq, h * HEAD_DIM), lambda bi, qt, n, pt_: (bi * nq + qt, 0)),
        scratch_shapes=[pltpu.VMEM((rows, LANES), F32), pltpu.VMEM((rows, LANES), F32), pltpu.VMEM((rows, LANES), F32),
                        pltpu.VMEM((max(n_past, 1), rows, HEAD_DIM), F32)],
    )
    assert n_past <= LANES
    return pl.pallas_call(
        functools.partial(_moba_kernel, tq=tq, pos0=pos0, n_past=n_past, paged=paged, own_keys=own_keys),
        grid_spec=grid_spec,
        out_shape=jax.ShapeDtypeStruct((b * t, h * HEAD_DIM), BF16),
        compiler_params=_cparams("parallel", "parallel", "arbitrary"),
    )(pt, p, *kv_args, p, p, bias_past, bias_own)


def _moba_layer(x, b, t, pos0, rel_bias, cache, layer, page_table, norm_mix, w_in, w_o):
    p = _mm(x, w_in.astype(BF16), norm_g=norm_mix)
    o = _moba_attend(p, b, t, pos0, rel_bias, cache, layer, page_table)
    kv = p[:, N_HEADS * HEAD_DIM:].reshape(b, t, 2, MOBA_KV_HEADS, HEAD_DIM)
    return _mm(o, w_o.astype(BF16), res=x), kv


def shared_key_bias(dist, rel_bias, groups):
    nq, nk = dist.shape
    b = rel_bias[rel_bucket(dist)].astype(jnp.float32)
    return b.reshape(nq, nk, groups, -1).transpose(0, 2, 3, 1)


def masked_softmax(logits, mask):
    z = jnp.where(mask, logits, NEG_INF)
    z = z - jnp.max(z, axis=-1, keepdims=True)
    e = jnp.where(mask, jnp.exp(z), 0.0)
    return e / jnp.maximum(jnp.sum(e, axis=-1, keepdims=True), 1e-30)


def nsa_compress(kv_c, w_cmp, b_cmp):
    n_chunk = kv_c.shape[0] // NSA_CMP_STRIDE
    parts = NSA_CMP_BLOCK // NSA_CMP_STRIDE
    x = kv_c[: n_chunk * NSA_CMP_STRIDE].reshape(n_chunk, NSA_CMP_STRIDE, 2, NSA_KV_HEADS, HEAD_DIM)
    w = w_cmp.reshape(2, parts, NSA_CMP_STRIDE, HEAD_DIM, HEAD_DIM)
    proj = jnp.einsum('cskgd,kpsde->pckge', x, w)
    n_cmp = n_chunk - parts + 1
    out = b_cmp[None, :, None, :] + proj[0, :n_cmp]
    for p in range(1, parts):
        out = out + proj[p, p:p + n_cmp]
    return out


def cmp_to_sel_map(n_cmp, n_sel):
    c0 = np.arange(n_cmp)[:, None] * NSA_CMP_STRIDE
    s0 = np.arange(n_sel)[None, :] * NSA_SEL_BLOCK
    inter = np.minimum(c0 + NSA_CMP_BLOCK, s0 + NSA_SEL_BLOCK) - np.maximum(c0, s0)
    return (np.maximum(inter, 0) / NSA_CMP_BLOCK).astype(np.float32)


def nsa_seq(q, gates, kv_cs, win_kv, q_pos0, win_pos0, w_cmp, b_cmp, rel_bias):
    tq = q.shape[0]
    seq_len = kv_cs.shape[0]
    G, HPG = NSA_KV_HEADS, N_HEADS // NSA_KV_HEADS
    dt = q.dtype
    cmp = nsa_compress(kv_cs[:, :2], w_cmp, b_cmp).astype(dt)
    n_cmp = cmp.shape[0]
    cmp_end = jnp.arange(n_cmp, dtype=jnp.int32) * NSA_CMP_STRIDE + (NSA_CMP_BLOCK - 1)
    n_sel = -(-seq_len // NSA_SEL_BLOCK)
    sel = jnp.pad(kv_cs[:, 2:], ((0, n_sel * NSA_SEL_BLOCK - seq_len), (0, 0), (0, 0), (0, 0)))
    k_sel, v_sel = sel[:, 0], sel[:, 1]
    imp_map = jnp.asarray(cmp_to_sel_map(n_cmp, n_sel))
    n_top = min(NSA_N_SEL, n_sel)
    qb = NSA_QBLOCK if tq % NSA_QBLOCK == 0 else tq
    n_win = NSA_WINDOW + qb
    win = jnp.pad(win_kv, ((NSA_WINDOW, 0), (0, 0), (0, 0), (0, 0)))
    tbl = rel_bias.reshape(N_BUCKETS, G, HPG)
    gi = jnp.arange(G)[None, :, None]
    scale = HEAD_DIM ** -0.5

    def block(inp):
        qblk, gblk, s = inp
        t0 = q_pos0 + s
        t = t0 + jnp.arange(qb, dtype=jnp.int32)
        qg = (qblk * scale).reshape(qb, G, HPG, HEAD_DIM)
        d_c = t[:, None] - cmp_end[None, :]
        lg = jnp.einsum('qghd,cgd->qghc', qg, cmp[:, 0]).astype(jnp.float32) + shared_key_bias(d_c, rel_bias, G)
        p_c = masked_softmax(lg, (d_c >= 0)[:, None, None, :])
        o_c = jnp.einsum('qghc,cgd->qghd', p_c.astype(dt), cmp[:, 1])
        imp = jnp.einsum('qgc,cs->qgs', p_c.sum(axis=2), imp_map)
        blk = jnp.arange(n_sel, dtype=jnp.int32)[None, :]
        tb = (t // NSA_SEL_BLOCK)[:, None]
        forced = (blk == 0) | (blk == tb) | (blk == tb - 1)
        score = jnp.where((blk <= tb)[:, None, :], jnp.where(forced[:, None, :], NSA_FORCE, imp), NEG_INF)
        top_s, top_i = lax.top_k(score, n_top)
        pos = (top_i[..., None] * NSA_SEL_BLOCK + jnp.arange(NSA_SEL_BLOCK, dtype=jnp.int32)).reshape(qb, G, -1)
        ok = jnp.repeat(top_s > NEG_INF / 2, NSA_SEL_BLOCK, axis=-1)
        k_g, v_g = k_sel[pos, gi], v_sel[pos, gi]
        d_s = t[:, None, None] - pos
        b_s = tbl[rel_bucket(d_s), gi].astype(jnp.float32).transpose(0, 1, 3, 2)
        lg = jnp.einsum('qghd,qgkd->qghk', qg, k_g).astype(jnp.float32) + b_s
        p_s = masked_softmax(lg, (ok & (d_s >= 0))[:, :, None, :])
        o_s = jnp.einsum('qghk,qgkd->qghd', p_s.astype(dt), v_g)
        wk = lax.dynamic_slice_in_dim(win, t0 - win_pos0, n_win, axis=0)
        kpos = t0 - NSA_WINDOW + jnp.arange(n_win, dtype=jnp.int32)
        d_w = t[:, None] - kpos[None, :]
        m_w = (d_w >= 0) & (d_w <= NSA_WINDOW) & (kpos >= win_pos0)[None, :]
        lg = jnp.einsum('qghd,kgd->qghk', qg, wk[:, 0]).astype(jnp.float32) + shared_key_bias(d_w, rel_bias, G)
        p_w = masked_softmax(lg, m_w[:, None, None, :])
        o_w = jnp.einsum('qghk,kgd->qghd', p_w.astype(dt), wk[:, 1])
        gg = gblk.reshape(qb, G, HPG, 3).astype(dt)
        o = gg[..., 0:1] * o_c + gg[..., 1:2] * o_s + gg[..., 2:3] * o_w
        return o.reshape(qb, N_HEADS, HEAD_DIM)

    nb = tq // qb
    starts = jnp.arange(nb, dtype=jnp.int32) * qb
    out = lax.map(block, (q.reshape(nb, qb, N_HEADS, HEAD_DIM), gates.reshape(nb, qb, N_HEADS, 3), starts))
    return out.reshape(tq, N_HEADS, HEAD_DIM)


def _nsa_layer(x, b, t, pos0, rel_bias, pool, layer, win_buf, page_table, norm_mix, w_in, b_gate, w_cmp, b_cmp, w_o):
    nq, nkv = N_HEADS * HEAD_DIM, 6 * NSA_KV_HEADS * HEAD_DIM
    p = _mm(x, _pad_cols(w_in, 2 * LANES).astype(BF16), norm_g=norm_mix)
    q = p[:, :nq].reshape(b, t, N_HEADS, HEAD_DIM)
    kv = p[:, nq:nq + nkv].reshape(b, t, 6, NSA_KV_HEADS, HEAD_DIM)
    gates = jax.nn.sigmoid(p[:, nq + nkv:nq + nkv + 3 * N_HEADS] + b_gate).reshape(b, t, N_HEADS, 3)
    if pool is None:
        def one(inp):
            qs, gs, kvs = inp
            return nsa_seq(qs, gs, kvs[:, :4], kvs[:, 4:], 0, 0, w_cmp, b_cmp, rel_bias)

        o = lax.map(one, (q, gates, kv))
        win_out = kv[:, t - min(NSA_WINDOW, t):, 4:]
    else:
        nbuf = win_buf.shape[1]
        win_all = jnp.concatenate([win_buf, kv[:, :, 4:]], axis=1)

        def one(inp):
            qs, gs, kvs, pages, ws = inp
            rows = pool[layer, pages].reshape((-1,) + pool.shape[3:])
            kv_cs = jnp.concatenate([rows, kvs[:, :4]], axis=0)
            return nsa_seq(qs, gs, kv_cs, ws, pos0, pos0 - nbuf, w_cmp, b_cmp, rel_bias)

        o = lax.map(one, (q, gates, kv, page_table, win_all))
        win_out = win_all[:, -nbuf:]
    y = _mm(o.reshape(b * t, nq).astype(BF16), w_o.astype(BF16), res=x)
    return y, kv[:, :, :4], win_out


def _ffn_layer(x, b, t, state, norm_ffn, w_in, conv_w, conv_b, w_out):
    gu = _mm(x, w_in.astype(BF16), norm_g=norm_ffn)
    a = _ffn_gate(gu, conv_w, conv_b, t, state)
    y = _mm(a, w_out.astype(BF16), res=x)
    g3 = gu[:, :D_FF].reshape(b, t, D_FF)
    if state is None:
        prev = jnp.zeros((b, CONV_W - 1, D_FF), F32)
    else:
        prev = state
    new_state = jnp.concatenate([prev, g3[:, max(t - (CONV_W - 1), 0):]], axis=1)[:, -(CONV_W - 1):]
    return y, new_state


def kernel(x_prompt, x_sample, state_gla, cache_nsa_kv, cache_nsa_win, cache_mla, cache_moba_kv, state_ffn_conv, page_table, rel_bias, norm_mix, norm_ffn, norm_final, ffn_w_in, ffn_conv_w, ffn_conv_b, ffn_w_out, gla_w_in, gla_b_r, gla_w_a1, gla_w_a2, gla_b_a, gla_norm, gla_w_o, nsa_w_in, nsa_b_gate, nsa_w_cmp, nsa_b_cmp, nsa_w_o, mla_w_down, mla_q_norm, mla_w_uq, mla_kv_norm, mla_w_uk, mla_w_uv, mla_w_o, moba_w_in, moba_w_o):
    bp, tp, d = x_prompt.shape
    bs, ts, _ = x_sample.shape
    past = page_table.shape[1] * PAGE_SIZE
    pos_p = jnp.arange(tp, dtype=jnp.int32)
    pos_s = past + jnp.arange(ts, dtype=jnp.int32)
    xp = x_prompt.reshape(bp * tp, d)
    xs = x_sample.reshape(bs * ts, d)
    outs = {k: [] for k in ('gla_p', 'gla_s', 'nkv_p', 'nkv_s', 'nwin_p', 'nwin_s', 'mla_p', 'mla_s', 'mkv_p', 'mkv_s',
                            'conv_p', 'conv_s')}
    for i in range(DEPTH):
        m, j = i % N_MIXERS, i // N_MIXERS
        if m == 0:
            w = (norm_mix[i], gla_w_in[j], gla_b_r[j], gla_w_a1[j], gla_w_a2[j], gla_b_a[j], gla_norm[j], gla_w_o[j])
            xp, st = _gla_layer(xp, bp, tp, None, *w)
            outs['gla_p'].append(st)
            xs, st = _gla_layer(xs, bs, ts, state_gla[j], *w)
            outs['gla_s'].append(st)
        elif m == 1:
            w = (norm_mix[i], nsa_w_in[j], nsa_b_gate[j], nsa_w_cmp[j], nsa_b_cmp[j], nsa_w_o[j])
            xp, kv, win = _nsa_layer(xp, bp, tp, 0, rel_bias, None, j, None, None, *w)
            outs['nkv_p'].append(kv)
            outs['nwin_p'].append(win)
            xs, kv, win = _nsa_layer(xs, bs, ts, past, rel_bias, cache_nsa_kv, j, cache_nsa_win[j], page_table, *w)
            outs['nkv_s'].append(kv)
            outs['nwin_s'].append(win)
        elif m == 2:
            w = (norm_mix[i], mla_w_down[j], mla_q_norm[j], mla_w_uq[j], mla_kv_norm[j], mla_w_uk[j], mla_w_uv[j],
                 mla_w_o[j])
            xp, lat = _mla_layer(xp, bp, tp, pos_p, None, j, None, *w)
            outs['mla_p'].append(lat)
            xs, lat = _mla_layer(xs, bs, ts, pos_s, cache_mla, j, page_table, *w)
            outs['mla_s'].append(lat)
        else:
            w = (norm_mix[i], moba_w_in[j], moba_w_o[j])
            xp, kv = _moba_layer(xp, bp, tp, 0, rel_bias, None, j, None, *w)
            outs['mkv_p'].append(kv)
            xs, kv = _moba_layer(xs, bs, ts, past, rel_bias, cache_moba_kv, j, page_table, *w)
            outs['mkv_s'].append(kv)
        wf = (norm_ffn[i], ffn_w_in[i], ffn_conv_w[i], ffn_conv_b[i], ffn_w_out[i])
        xp, cst = _ffn_layer(xp, bp, tp, None, *wf)
        outs['conv_p'].append(cst)
        xs, cst = _ffn_layer(xs, bs, ts, state_ffn_conv[i], *wf)
        outs['conv_s'].append(cst)
    y_prompt = _rms(xp, norm_final).reshape(bp, tp, d)
    y_sample = _rms(xs, norm_final).reshape(bs, ts, d)
    return (y_prompt, y_sample) + tuple(jnp.stack(outs[k]) for k in (
        'gla_p', 'gla_s', 'nkv_p', 'nkv_s', 'nwin_p', 'nwin_s', 'mla_p', 'mla_s', 'mkv_p', 'mkv_s', 'conv_p', 'conv_s'))
```

```python
import functools
import math

import numpy as np
import jax
import jax.numpy as jnp
from jax import lax
from jax.experimental import pallas as pl
from jax.experimental.pallas import tpu as pltpu

F32 = jnp.float32
BF16 = jnp.bfloat16

D_MODEL = 2048
DEPTH = 4
PAGE_SIZE = 128
N_MIXERS = 4
N_HEADS = 16
HEAD_DIM = D_MODEL // N_HEADS
N_BUCKETS = 32
MAX_DISTANCE = 128
RMS_EPS = 1e-6
NEG_INF = -1e30

GLA_HEADS = 4
GLA_DK = D_MODEL // 2 // GLA_HEADS
GLA_DV = D_MODEL // GLA_HEADS
GLA_GATE_RANK = 16
GLA_TAU = 16.0
GLA_CHUNK = 64

NSA_KV_HEADS = 2
NSA_CMP_BLOCK = 32
NSA_CMP_STRIDE = 16
NSA_SEL_BLOCK = 64
NSA_N_SEL = 16
NSA_WINDOW = 512
NSA_QBLOCK = 64
NSA_FORCE = 1e9

MLA_Q_RANK = 512
MLA_KV_RANK = 512
MLA_NOPE = 128
MLA_ROPE = 64
MLA_V = 128
ROPE_THETA = 10000.0

MOBA_KV_HEADS = 4
MOBA_BLOCK = 256
MOBA_TOPK = 3

D_FF = 5632
CONV_W = 3

V7X_VMEM_BYTES = 64 * 1024 * 1024
VMEM_LIMIT_BYTES = V7X_VMEM_BYTES * 3 // 4
LANES = 128
SUBLANES_BF16 = 16


def _cparams(*sem):
    return pltpu.CompilerParams(dimension_semantics=sem, vmem_limit_bytes=VMEM_LIMIT_BYTES)


def _pick(n, cap, mult):
    best = None
    for d in range(mult, min(n, cap) + 1, mult):
        if n % d == 0:
            best = d
    assert best is not None, (n, cap, mult)
    return best


def _pad_cols(w, mult=LANES):
    n = w.shape[-1]
    pad = (-n) % mult
    if pad:
        w = jnp.pad(w, [(0, 0)] * (w.ndim - 1) + [(0, pad)])
    return w


def _dot(a, b):
    return jnp.dot(a, b, preferred_element_type=F32)


def _dot_nt(a, b):
    return lax.dot_general(a, b, (((1,), (1,)), ((), ())), preferred_element_type=F32)


def _dot_tn(a, b):
    return lax.dot_general(a, b, (((0,), (0,)), ((), ())), preferred_element_type=F32)


def _sigmoid(x):
    return 1.0 / (1.0 + jnp.exp(-x))


def _mm_kernel(*refs, has_norm, has_res):
    it = iter(refs)
    x_ref, w_ref = next(it), next(it)
    g_ref = next(it) if has_norm else None
    r_ref = next(it) if has_res else None
    o_ref = next(it)
    if has_norm:
        h_ref = next(it)

        @pl.when(pl.program_id(1) == 0)
        def _():
            xf = x_ref[...].astype(F32)
            y = xf * lax.rsqrt(jnp.mean(xf * xf, axis=-1, keepdims=True) + RMS_EPS)
            h_ref[...] = (y * g_ref[...]).astype(BF16)

        lhs = h_ref[...]
    else:
        lhs = x_ref[...].astype(BF16)
    acc = _dot(lhs, w_ref[...])
    if has_res:
        acc = acc + r_ref[...]
    o_ref[...] = acc.astype(o_ref.dtype)


def _mm(x, w, *, x_col=0, norm_g=None, res=None, out_dtype=F32, tm_cap=1024, tn_cap=1024):
    m = x.shape[0]
    k, n = w.shape
    assert n % LANES == 0
    tm = _pick(m, tm_cap, SUBLANES_BF16)
    tn = n if n <= 1536 else _pick(n, tn_cap, LANES)
    if k > 4096:
        tm = _pick(m, 512, SUBLANES_BF16)
    in_specs = [pl.BlockSpec((tm, k), lambda i, j: (i, x_col)),
                pl.BlockSpec((k, tn), lambda i, j: (0, j))]
    args = [x, w]
    scratch = []
    if norm_g is not None:
        in_specs.append(pl.BlockSpec((1, k), lambda i, j: (0, 0)))
        args.append(norm_g.reshape(1, k).astype(F32))
        scratch.append(pltpu.VMEM((tm, k), BF16))
    if res is not None:
        in_specs.append(pl.BlockSpec((tm, tn), lambda i, j: (i, j)))
        args.append(res)
    return pl.pallas_call(
        functools.partial(_mm_kernel, has_norm=norm_g is not None, has_res=res is not None),
        grid=(m // tm, n // tn),
        in_specs=in_specs,
        out_specs=pl.BlockSpec((tm, tn), lambda i, j: (i, j)),
        out_shape=jax.ShapeDtypeStruct((m, n), out_dtype),
        scratch_shapes=scratch,
        compiler_params=_cparams("parallel", "arbitrary"),
    )(*args)


def _rms_kernel(x_ref, g_ref, o_ref):
    xf = x_ref[...]
    y = xf * lax.rsqrt(jnp.mean(xf * xf, axis=-1, keepdims=True) + RMS_EPS)
    o_ref[...] = y * g_ref[...]


def _rms(x, g):
    m, d = x.shape
    tm = _pick(m, 512, 8)
    return pl.pallas_call(
        _rms_kernel,
        grid=(m // tm,),
        in_specs=[pl.BlockSpec((tm, d), lambda i: (i, 0)), pl.BlockSpec((1, d), lambda i: (0, 0))],
        out_specs=pl.BlockSpec((tm, d), lambda i: (i, 0)),
        out_shape=jax.ShapeDtypeStruct((m, d), F32),
        compiler_params=_cparams("parallel"),
    )(x, g.reshape(1, d))


def _gate_kernel(*refs, seq_len, tr, has_state):
    if has_state:
        g_ref, u_ref, gp_ref, e1_ref, e2_ref, cw_ref, cb_ref, o_ref = refs
    else:
        g_ref, u_ref, gp_ref, cw_ref, cb_ref, o_ref = refs
    g = g_ref[...]
    gp = gp_ref[...]
    rl = lax.broadcasted_iota(jnp.int32, (tr, 1), 0)
    g1 = jnp.where(rl == 0, gp[7:8], pltpu.roll(g, 1, axis=0))
    g2 = jnp.where(rl == 0, gp[6:7], jnp.where(rl == 1, gp[7:8], pltpu.roll(g, 2, axis=0)))
    if seq_len >= tr:
        pos = lax.rem(pl.program_id(0) * tr, seq_len) + rl
    else:
        pos = lax.rem(rl, seq_len)
    if has_state:
        e1, e2 = e1_ref[...], e2_ref[...]
    else:
        e1 = e2 = jnp.zeros_like(g)
    g1 = jnp.where(pos >= 1, g1, e1)
    g2 = jnp.where(pos >= 2, g2, e2)
    cw = cw_ref[...]
    gc = cb_ref[...] + g2 * cw[0:1] + g1 * cw[1:2] + g * cw[2:3]
    o_ref[...] = (gc * _sigmoid(gc) * u_ref[...]).astype(o_ref.dtype)


def _ffn_gate(gu, conv_w, conv_b, seq_len, state):
    m = gu.shape[0]
    tc = _pick(D_FF, 1024, LANES)
    ncb = D_FF // tc
    tr = _pick(seq_len, 512, 8) if seq_len >= 16 else _pick(m, 512, 8 * seq_len // math.gcd(8, seq_len))
    if seq_len < tr:
        assert tr % seq_len == 0 and seq_len >= CONV_W - 1
    else:
        assert seq_len % tr == 0
    in_specs = [pl.BlockSpec((tr, tc), lambda i, j: (i, j)),
                pl.BlockSpec((tr, tc), lambda i, j: (i, j + ncb)),
                pl.BlockSpec((8, tc), lambda i, j: (jnp.maximum(i * (tr // 8) - 1, 0), j))]
    args = [gu, gu, gu]
    if state is not None:
        b = state.shape[0]
        z = jnp.zeros((b, seq_len, D_FF), F32)
        e1 = z.at[:, 0].set(state[:, 1]).reshape(m, D_FF)
        e2 = z.at[:, 0].set(state[:, 0]).at[:, 1].set(state[:, 1]).reshape(m, D_FF)
        in_specs += [pl.BlockSpec((tr, tc), lambda i, j: (i, j))] * 2
        args += [e1, e2]
    in_specs += [pl.BlockSpec((8, tc), lambda i, j: (0, j)), pl.BlockSpec((1, tc), lambda i, j: (0, j))]
    args += [jnp.pad(conv_w, ((0, 8 - CONV_W), (0, 0))), conv_b.reshape(1, D_FF)]
    return pl.pallas_call(
        functools.partial(_gate_kernel, seq_len=seq_len, tr=tr, has_state=state is not None),
        grid=(m // tr, ncb),
        in_specs=in_specs,
        out_specs=pl.BlockSpec((tr, tc), lambda i, j: (i, j)),
        out_shape=jax.ShapeDtypeStruct((m, D_FF), BF16),
        compiler_params=_cparams("parallel", "parallel"),
    )(*args)


def _cumsum_rows(x):
    c = x.shape[0]
    row = lax.broadcasted_iota(jnp.int32, x.shape, 0)
    s = 1
    while s < c:
        x = x + jnp.where(row >= s, pltpu.roll(x, s, axis=0), 0.0)
        s *= 2
    return x


def _gla_kernel(*refs, c, has_state):
    it = iter(refs)
    q_ref, k_ref, v_ref, r_ref, ga_ref, wa2_ref, ba_ref, ng_ref, br_ref = (next(it) for _ in range(9))
    s0_ref = next(it) if has_state else None
    o_ref, sout_ref, st_ref = next(it), next(it), next(it)
    ci = pl.program_id(2)

    @pl.when(ci == 0)
    def _():
        if has_state:
            st_ref[...] = s0_ref[...].T
        else:
            st_ref[...] = jnp.zeros_like(st_ref)

    q = q_ref[...] * (GLA_DK ** -0.5)
    k = k_ref[...]
    v16 = v_ref[...].astype(BF16)
    lr = _dot(ga_ref[...].astype(BF16), wa2_ref[...]) + ba_ref[...]
    log_a = (jnp.minimum(lr, 0.0) - jnp.log(1.0 + jnp.exp(-jnp.abs(lr)))) / GLA_TAU
    cum = _cumsum_rows(log_a)
    q_dec = (q * jnp.exp(cum)).astype(BF16)
    k_inv = (k * jnp.exp(-cum)).astype(BF16)
    att = _dot_nt(q_dec, k_inv)
    ri = lax.broadcasted_iota(jnp.int32, (c, c), 0)
    cj = lax.broadcasted_iota(jnp.int32, (c, c), 1)
    att = jnp.where(ri >= cj, att, 0.0)
    st = st_ref[...]
    o = _dot(att.astype(BF16), v16) + _dot_nt(q_dec, st.astype(BF16))
    last = cum[c - 1:c]
    k_dec = (k * jnp.exp(last - cum)).astype(BF16)
    st_new = st * jnp.exp(last) + _dot_tn(v16, k_dec)
    st_ref[...] = st_new
    on = o * lax.rsqrt(jnp.mean(o * o, axis=-1, keepdims=True) + RMS_EPS) * ng_ref[...]
    rr = r_ref[...] + br_ref[...]
    o_ref[...] = (on * (rr * _sigmoid(rr))).astype(o_ref.dtype)

    @pl.when(ci == pl.num_programs(2) - 1)
    def _():
        sout_ref[...] = st_new.T


def _gla_scan(p, b, t, w_a2p, b_a, norm_g, b_r, s0):
    c = GLA_CHUNK if t % GLA_CHUNK == 0 else t
    nch = t // c
    h = GLA_HEADS
    nk = h * GLA_DK
    kq, kv_ = GLA_DK, GLA_DV
    row = lambda bi, hi, ci: bi * nch + ci
    in_specs = [
        pl.BlockSpec((c, kq), lambda bi, hi, ci: (row(bi, hi, ci), hi)),
        pl.BlockSpec((c, kq), lambda bi, hi, ci: (row(bi, hi, ci), h + hi)),
        pl.BlockSpec((c, kv_), lambda bi, hi, ci: (row(bi, hi, ci), 2 * nk // kv_ + hi)),
        pl.BlockSpec((c, kv_), lambda bi, hi, ci: (row(bi, hi, ci), 2 * nk // kv_ + h + hi)),
        pl.BlockSpec((c, LANES), lambda bi, hi, ci: (row(bi, hi, ci), (2 * nk + 2 * h * kv_) // LANES)),
        pl.BlockSpec((LANES, kq), lambda bi, hi, ci: (0, hi)),
        pl.BlockSpec((1, kq), lambda bi, hi, ci: (0, hi)),
        pl.BlockSpec((1, kv_), lambda bi, hi, ci: (0, 0)),
        pl.BlockSpec((1, kv_), lambda bi, hi, ci: (0, hi)),
    ]
    args = [p, p, p, p, p, w_a2p, b_a.reshape(1, nk), norm_g.reshape(1, kv_), b_r.reshape(1, h * kv_)]
    if s0 is not None:
        in_specs.append(pl.BlockSpec((None, None, kq, kv_), lambda bi, hi, ci: (bi, hi, 0, 0)))
        args.append(s0)
    return pl.pallas_call(
        functools.partial(_gla_kernel, c=c, has_state=s0 is not None),
        grid=(b, h, nch),
        in_specs=in_specs,
        out_specs=[pl.BlockSpec((c, kv_), lambda bi, hi, ci: (row(bi, hi, ci), hi)),
                   pl.BlockSpec((None, None, kq, kv_), lambda bi, hi, ci: (bi, hi, 0, 0))],
        out_shape=[jax.ShapeDtypeStruct((b * t, h * kv_), BF16),
                   jax.ShapeDtypeStruct((b, h, kq, kv_), F32)],
        scratch_shapes=[pltpu.VMEM((kv_, kq), F32)],
        compiler_params=_cparams("parallel", "parallel", "arbitrary"),
    )(*args)


def _gla_layer(x, b, t, s0, norm_mix, w_in, b_r, w_a1, w_a2, b_a, norm_g, w_o):
    w_cat = jnp.concatenate([w_in, _pad_cols(w_a1)], axis=1).astype(BF16)
    p = _mm(x, w_cat, norm_g=norm_mix)
    w_a2p = jnp.pad(w_a2, ((0, LANES - GLA_GATE_RANK), (0, 0))).astype(BF16)
    o, s_new = _gla_scan(p, b, t, w_a2p, b_a, norm_g, b_r, s0)
    return _mm(o, w_o.astype(BF16), res=x), s_new


def _rel_bucket_np(dist):
    n = np.maximum(dist, 0)
    exact = N_BUCKETS // 2
    lg = np.log(np.maximum(n, 1).astype(np.float32) / np.float32(exact)) / np.float32(math.log(MAX_DISTANCE / exact))
    large = np.minimum(exact + (lg * np.float32(N_BUCKETS - exact)).astype(np.int32), N_BUCKETS - 1)
    return np.where(n < exact, n, large)


def rel_bucket(dist):
    n = jnp.maximum(dist, 0)
    exact = N_BUCKETS // 2
    lg = jnp.log(jnp.maximum(n, 1).astype(jnp.float32) / exact) / math.log(MAX_DISTANCE / exact)
    large = jnp.minimum(exact + (lg * (N_BUCKETS - exact)).astype(jnp.int32), N_BUCKETS - 1)
    return jnp.where(n < exact, n, large)


def _toeplitz_bias(rel_bias, tq, tk, c):
    d = c + jnp.arange(tq, dtype=jnp.int32)[:, None] - jnp.arange(tk, dtype=jnp.int32)[None, :]
    b = rel_bias[rel_bucket(d)].astype(F32)
    return b.transpose(2, 0, 1).reshape(N_HEADS * tq, tk)


def _rope_tables(pos):
    half = MLA_ROPE // 2
    inv = ROPE_THETA ** (-jnp.arange(half, dtype=jnp.float32) / half)
    ang = pos.astype(jnp.float32)[:, None] * inv
    cos, sin = jnp.cos(ang), jnp.sin(ang)
    z = jnp.zeros_like(cos)
    zz = jnp.zeros((pos.shape[0], LANES - MLA_ROPE), F32)
    c = jnp.concatenate([cos, cos, zz], axis=1)
    sa = jnp.concatenate([-sin, z, zz], axis=1)
    sb = jnp.concatenate([z, sin, zz], axis=1)
    return c, sa, sb


def _rope128(x, c, sa, sb):
    return x * c + pltpu.roll(x, LANES - MLA_ROPE // 2, axis=1) * sa + pltpu.roll(x, MLA_ROPE // 2, axis=1) * sb


def _mla_latent_kernel(ckv_ref, kpe_ref, g_ref, c_ref, sa_ref, sb_ref, o_ref):
    x = ckv_ref[...]
    y = x * lax.rsqrt(jnp.mean(x * x, axis=-1, keepdims=True) + RMS_EPS) * g_ref[...]
    kp = _rope128(kpe_ref[...], c_ref[...], sa_ref[...], sb_ref[...])
    o_ref[:, 0:MLA_KV_RANK] = y
    o_ref[:, MLA_KV_RANK:MLA_KV_RANK + MLA_ROPE] = kp[:, 0:MLA_ROPE]


def _mla_q_kernel(xn_ref, xp_ref, w_ref, c_ref, sa_ref, sb_ref, qa_ref, qp_ref):
    qa_ref[...] = _dot(xn_ref[...].astype(BF16), w_ref[...]).astype(qa_ref.dtype)
    qp_ref[...] = _rope128(xp_ref[...], c_ref[...], sa_ref[...], sb_ref[...]).astype(qp_ref.dtype)


def _mla_project(x, t, pos, norm_mix, w_down, q_norm, w_uq, kv_norm, w_uk):
    m = x.shape[0]
    h = N_HEADS
    d = _mm(x, _pad_cols(w_down).astype(BF16), norm_g=norm_mix)
    w3 = w_uq.reshape(MLA_Q_RANK, h, MLA_NOPE + MLA_ROPE)
    w_n = w3[:, :, :MLA_NOPE].reshape(MLA_Q_RANK, h * MLA_NOPE)
    w_p = jnp.pad(w3[:, :, MLA_NOPE:], ((0, 0), (0, 0), (0, LANES - MLA_ROPE))).reshape(MLA_Q_RANK, h * LANES)
    q = _mm(d, jnp.concatenate([w_n, w_p], axis=1).astype(BF16), norm_g=q_norm)
    tabs = _rope_tables(pos)
    period = pos.shape[0]
    tm = _pick(m, 512, SUBLANES_BF16)
    if period < tm:
        assert tm % period == 0
        tabs = tuple(jnp.tile(tb, (tm // period, 1)) for tb in tabs)
        ntab = 1
    else:
        assert period % tm == 0
        ntab = period // tm
    latent = pl.pallas_call(
        _mla_latent_kernel,
        grid=(m // tm,),
        in_specs=[pl.BlockSpec((tm, MLA_KV_RANK), lambda i: (i, 1)),
                  pl.BlockSpec((tm, LANES), lambda i: (i, (MLA_Q_RANK + MLA_KV_RANK) // LANES)),
                  pl.BlockSpec((1, MLA_KV_RANK), lambda i: (0, 0))]
        + [pl.BlockSpec((tm, LANES), lambda i: (i % ntab, 0))] * 3,
        out_specs=pl.BlockSpec((tm, MLA_KV_RANK + MLA_ROPE), lambda i: (i, 0)),
        out_shape=jax.ShapeDtypeStruct((m, MLA_KV_RANK + MLA_ROPE), F32),
        compiler_params=_cparams("parallel"),
    )(d, d, kv_norm.reshape(1, MLA_KV_RANK), *tabs)
    w_ukt = w_uk.transpose(1, 2, 0).astype(BF16)
    qa, qp = pl.pallas_call(
        _mla_q_kernel,
        grid=(m // tm, h),
        in_specs=[pl.BlockSpec((tm, MLA_NOPE), lambda i, hi: (i, hi)),
                  pl.BlockSpec((tm, LANES), lambda i, hi: (i, h + hi)),
                  pl.BlockSpec((None, MLA_NOPE, MLA_KV_RANK), lambda i, hi: (hi, 0, 0))]
        + [pl.BlockSpec((tm, LANES), lambda i, hi: (i % ntab, 0))] * 3,
        out_specs=[pl.BlockSpec((None, tm, MLA_KV_RANK), lambda i, hi: (hi, i, 0)),
                   pl.BlockSpec((None, tm, LANES), lambda i, hi: (hi, i, 0))],
        out_shape=[jax.ShapeDtypeStruct((h, m, MLA_KV_RANK), BF16),
                   jax.ShapeDtypeStruct((h, m, LANES), BF16)],
        compiler_params=_cparams("parallel", "parallel"),
    )(q, q, w_ukt, *tabs)
    return qa, qp, latent


def _softmax_step(s, mask, m_ref, l_ref):
    z = jnp.where(mask, s, NEG_INF) if mask is not None else s
    m_old = m_ref[...]
    m_new = jnp.maximum(m_old, jnp.max(z, axis=-1, keepdims=True))
    p = jnp.exp(z - m_new)
    if mask is not None:
        p = jnp.where(mask, p, 0.0)
    alpha = jnp.exp(m_old - m_new)
    l_ref[...] = alpha * l_ref[...] + jnp.sum(p, axis=-1, keepdims=True)
    m_ref[...] = m_new
    return alpha, p


def _mla_finish(acc_ref, l_ref, wuv_ref, o_ref, tq):
    o_lat = (acc_ref[...] / jnp.maximum(l_ref[...], 1e-30)).astype(BF16)
    for hi in range(N_HEADS):
        o_ref[:, hi * MLA_V:(hi + 1) * MLA_V] = _dot(o_lat[hi * tq:(hi + 1) * tq], wuv_ref[hi]).astype(o_ref.dtype)


MLA_SCALE = (MLA_NOPE + MLA_ROPE) ** -0.5


def _mla_prompt_kernel(qa_ref, qp_ref, lat_ref, wuv_ref, o_ref, m_ref, l_ref, acc_ref, *, tq, tk):
    qt, kt = pl.program_id(1), pl.program_id(2)
    rows = N_HEADS * tq

    @pl.when(kt == 0)
    def _():
        m_ref[...] = jnp.full_like(m_ref, NEG_INF)
        l_ref[...] = jnp.zeros_like(l_ref)
        acc_ref[...] = jnp.zeros_like(acc_ref)

    @pl.when(kt * tk <= qt * tq + tq - 1)
    def _():
        kc = lat_ref[:, 0:MLA_KV_RANK].astype(BF16)
        kp = lat_ref[:, MLA_KV_RANK:MLA_KV_RANK + MLA_ROPE].astype(BF16)
        qa = qa_ref[...].reshape(rows, MLA_KV_RANK)
        qp = qp_ref[...].reshape(rows, LANES)[:, 0:MLA_ROPE]
        s = (_dot_nt(qa, kc) + _dot_nt(qp, kp)) * MLA_SCALE
        ti = qt * tq + lax.rem(lax.broadcasted_iota(jnp.int32, (rows, tk), 0), tq)
        kj = kt * tk + lax.broadcasted_iota(jnp.int32, (rows, tk), 1)
        alpha, p = _softmax_step(s, kj <= ti, m_ref, l_ref)
        acc_ref[...] = alpha * acc_ref[...] + _dot(p.astype(BF16), kc)

    @pl.when(kt == pl.num_programs(2) - 1)
    def _():
        _mla_finish(acc_ref, l_ref, wuv_ref, o_ref, tq)


def _mla_prompt_attend(qa, qp, latent, b, t, w_uv16):
    h = N_HEADS
    tq = _pick(t, 128, SUBLANES_BF16)
    tk = _pick(t, 512, 8)
    nq, nk = t // tq, t // tk
    rows = h * tq

    def kmap(bi, qt, kt):
        return (bi * nk + jnp.minimum(kt, (qt * tq + tq - 1) // tk), 0)

    return pl.pallas_call(
        functools.partial(_mla_prompt_kernel, tq=tq, tk=tk),
        grid=(b, nq, nk),
        in_specs=[pl.BlockSpec((h, tq, MLA_KV_RANK), lambda bi, qt, kt: (0, bi * nq + qt, 0)),
                  pl.BlockSpec((h, tq, LANES), lambda bi, qt, kt: (0, bi * nq + qt, 0)),
                  pl.BlockSpec((tk, MLA_KV_RANK + MLA_ROPE), kmap),
                  pl.BlockSpec((h, MLA_KV_RANK, MLA_V), lambda bi, qt, kt: (0, 0, 0))],
        out_specs=pl.BlockSpec((tq, h * MLA_V), lambda bi, qt, kt: (bi * nq + qt, 0)),
        out_shape=jax.ShapeDtypeStruct((b * t, h * MLA_V), BF16),
        scratch_shapes=[pltpu.VMEM((rows, 1), F32), pltpu.VMEM((rows, 1), F32), pltpu.VMEM((rows, MLA_KV_RANK), F32)],
        compiler_params=_cparams("parallel", "parallel", "arbitrary"),
    )(qa, qp, latent, w_uv16)


def _mla_sample_kernel(pt_ref, qa_ref, qp_ref, *rest, npp, ts):
    page_refs = rest[:npp]
    new_ref, wuv_ref, o_ref, m_ref, l_ref, acc_ref = rest[npp:]
    st = pl.program_id(1)
    rows = N_HEADS * ts

    @pl.when(st == 0)
    def _():
        m_ref[...] = jnp.full_like(m_ref, NEG_INF)
        l_ref[...] = jnp.zeros_like(l_ref)
        acc_ref[...] = jnp.zeros_like(acc_ref)

    qa = qa_ref[...]
    qp = qp_ref[:, 0:MLA_ROPE]
    kcs = [r[0:MLA_KV_RANK, :].astype(BF16) for r in page_refs]
    kps = [r[MLA_KV_RANK:MLA_KV_RANK + MLA_ROPE, :].astype(BF16) for r in page_refs]
    s = jnp.concatenate([_dot(qa, kc) + _dot(qp, kp) for kc, kp in zip(kcs, kps)], axis=1) * MLA_SCALE
    alpha, p = _softmax_step(s, None, m_ref, l_ref)
    p16 = p.astype(BF16)
    pv = _dot_nt(p16[:, 0:PAGE_SIZE], kcs[0])
    for j in range(1, npp):
        pv = pv + _dot_nt(p16[:, j * PAGE_SIZE:(j + 1) * PAGE_SIZE], kcs[j])
    acc_ref[...] = alpha * acc_ref[...] + pv

    @pl.when(st == pl.num_programs(1) - 1)
    def _():
        kc = new_ref[:, 0:MLA_KV_RANK].astype(BF16)
        kp = new_ref[:, MLA_KV_RANK:MLA_KV_RANK + MLA_ROPE].astype(BF16)
        s2 = (_dot_nt(qa, kc) + _dot_nt(qp, kp)) * MLA_SCALE
        ti = lax.rem(lax.broadcasted_iota(jnp.int32, (rows, ts), 0), ts)
        kj = lax.broadcasted_iota(jnp.int32, (rows, ts), 1)
        alpha2, p2 = _softmax_step(s2, kj <= ti, m_ref, l_ref)
        acc_ref[...] = alpha2 * acc_ref[...] + _dot(p2.astype(BF16), kc)
        _mla_finish(acc_ref, l_ref, wuv_ref, o_ref, ts)


def _mla_sample_attend(qa, qp, latent, cache, layer, page_table, b, ts, w_uv16):
    h = N_HEADS
    npg = page_table.shape[1]
    npp = _pick(npg, 8, 1)
    nst = npg // npp
    rows = h * ts
    qa_s = qa.reshape(h, b, ts, MLA_KV_RANK).transpose(1, 0, 2, 3).reshape(b, rows, MLA_KV_RANK)
    qp_s = qp.reshape(h, b, ts, LANES).transpose(1, 0, 2, 3).reshape(b, rows, LANES)
    lat3 = latent.reshape(b, ts, MLA_KV_RANK + MLA_ROPE)
    cache_t = cache.transpose(0, 1, 3, 2)

    def page_map(j):
        return lambda bi, st, pt: (layer, pt[bi * npg + st * npp + j], 0, 0)

    grid_spec = pltpu.PrefetchScalarGridSpec(
        num_scalar_prefetch=1,
        grid=(b, nst),
        in_specs=[pl.BlockSpec((None, rows, MLA_KV_RANK), lambda bi, st, pt: (bi, 0, 0)),
                  pl.BlockSpec((None, rows, LANES), lambda bi, st, pt: (bi, 0, 0))]
        + [pl.BlockSpec((None, None, MLA_KV_RANK + MLA_ROPE, PAGE_SIZE), page_map(j)) for j in range(npp)]
        + [pl.BlockSpec((None, ts, MLA_KV_RANK + MLA_ROPE), lambda bi, st, pt: (bi, 0, 0)),
           pl.BlockSpec((h, MLA_KV_RANK, MLA_V), lambda bi, st, pt: (0, 0, 0))],
        out_specs=pl.BlockSpec((None, ts, h * MLA_V), lambda bi, st, pt: (bi, 0, 0)),
        scratch_shapes=[pltpu.VMEM((rows, 1), F32), pltpu.VMEM((rows, 1), F32), pltpu.VMEM((rows, MLA_KV_RANK), F32)],
    )
    out = pl.pallas_call(
        functools.partial(_mla_sample_kernel, npp=npp, ts=ts),
        grid_spec=grid_spec,
        out_shape=jax.ShapeDtypeStruct((b, ts, h * MLA_V), BF16),
        compiler_params=_cparams("parallel", "arbitrary"),
    )(page_table.reshape(-1), qa_s, qp_s, *([cache_t] * npp), lat3, w_uv16)
    return out.reshape(b * ts, h * MLA_V)


def _mla_layer(x, b, t, pos, cache, layer, page_table, norm_mix, w_down, q_norm, w_uq, kv_norm, w_uk, w_uv, w_o):
    qa, qp, latent = _mla_project(x, t, pos, norm_mix, w_down, q_norm, w_uq, kv_norm, w_uk)
    w_uv16 = w_uv.transpose(1, 0, 2).astype(BF16)
    if cache is None:
        o = _mla_prompt_attend(qa, qp, latent, b, t, w_uv16)
    else:
        o = _mla_sample_attend(qa, qp, latent, cache, layer, page_table, b, t, w_uv16)
    return _mm(o, w_o.astype(BF16), res=x), latent.reshape(b, t, MLA_KV_RANK + MLA_ROPE)


MOBA_SCALE = HEAD_DIM ** -0.5
MOBA_HPG = N_HEADS // MOBA_KV_HEADS


def _group_rows(q_ref, g, hpg):
    return jnp.concatenate([q_ref[:, (g * hpg + j) * HEAD_DIM:(g * hpg + j + 1) * HEAD_DIM] for j in range(hpg)],
                           axis=0)


def _lane_put(ref, rows, n, col):
    cur = ref[rows, :]
    lane = lax.broadcasted_iota(jnp.int32, cur.shape, 1)
    ref[rows, :] = jnp.where(lane == n, col, cur)


def _moba_kernel(pt_ref, q_ref, *rest, tq, pos0, n_past, paged, own_keys, bps):
    ppb = MOBA_BLOCK // PAGE_SIZE
    nkv = ppb * bps if paged else 2
    kv_refs = rest[:nkv]
    own_refs = rest[nkv:nkv + 2]
    bp_ref, bo_ref, o_ref, m_all, l_all, g_all, o_all = rest[nkv + 2:]
    qt, n = pl.program_id(1), pl.program_id(2)
    hpg, grp = MOBA_HPG, MOBA_KV_HEADS
    gw = grp * HEAD_DIM
    rg = hpg * tq
    t0 = pos0 + qt * tq
    own = t0 // MOBA_BLOCK

    @pl.when(n == 0)
    def _():
        m_all[...] = jnp.full_like(m_all, NEG_INF)
        g_all[...] = jnp.full_like(g_all, NEG_INF)
        l_all[...] = jnp.zeros_like(l_all)
        o_all[...] = jnp.zeros_like(o_all)

    @pl.when(jnp.logical_and(n < n_past // bps, n * bps < own))
    def _():
        for i in range(bps):
            blk = n * bps + i
            bidx = jnp.where(blk == n_past - 1, 1, 0) if paged else 0
            for g in range(grp):
                rows = slice(g * rg, (g + 1) * rg)
                qg = _group_rows(q_ref, g, hpg)
                if paged:
                    prefs = kv_refs[i * ppb:(i + 1) * ppb]
                    kg = jnp.concatenate([r[pl.ds(g, PAGE_SIZE, stride=2 * grp), :] for r in prefs], axis=0)
                    vg = jnp.concatenate([r[pl.ds(grp + g, PAGE_SIZE, stride=2 * grp), :] for r in prefs], axis=0)
                else:
                    kg = kv_refs[0][:, g * HEAD_DIM:(g + 1) * HEAD_DIM]
                    vg = kv_refs[1][:, g * HEAD_DIM:(g + 1) * HEAD_DIM]
                kg, vg = kg.astype(BF16), vg.astype(BF16)
                both = _dot_nt(jnp.concatenate([qg, qg * MOBA_SCALE], axis=0).astype(BF16), kg)
                gate = jnp.sum(both[0:rg], axis=-1, keepdims=True) * (1.0 / MOBA_BLOCK)
                s = both[rg:2 * rg] + bp_ref[bidx, rows, :]
                mx = jnp.max(s, axis=-1, keepdims=True)
                p = jnp.exp(s - mx)
                _lane_put(m_all, rows, blk, mx)
                _lane_put(l_all, rows, blk, jnp.sum(p, axis=-1, keepdims=True))
                _lane_put(g_all, rows, blk, gate)
                o_all[blk, rows, :] = _dot(p.astype(BF16), vg)

    @pl.when(n == pl.num_programs(2) - 1)
    def _():
        ko, vo = own_refs[0][...], own_refs[1][...]
        nko = bo_ref.shape[1]
        if own_keys < nko:
            zpad = jnp.zeros((nko - own_keys, gw), F32)
            ko = jnp.concatenate([ko, zpad], axis=0)
            vo = jnp.concatenate([vo, zpad], axis=0)
        ti = t0 + lax.rem(lax.broadcasted_iota(jnp.int32, (rg, nko), 0), tq)
        kj = own * MOBA_BLOCK + lax.broadcasted_iota(jnp.int32, (rg, nko), 1)
        mask = kj <= ti
        lane = lax.broadcasted_iota(jnp.int32, (rg, LANES), 1)
        for g in range(grp):
            rows = slice(g * rg, (g + 1) * rg)
            qg = _group_rows(q_ref, g, hpg)
            kg = ko[:, g * HEAD_DIM:(g + 1) * HEAD_DIM].astype(BF16)
            vg = vo[:, g * HEAD_DIM:(g + 1) * HEAD_DIM].astype(BF16)
            s = _dot_nt((qg * MOBA_SCALE).astype(BF16), kg) + bo_ref[rows, :]
            z = jnp.where(mask, s, NEG_INF)
            m_o = jnp.max(z, axis=-1, keepdims=True)
            p = jnp.where(mask, jnp.exp(z - m_o), 0.0)
            l_o = jnp.sum(p, axis=-1, keepdims=True)
            o_o = _dot(p.astype(BF16), vg)
            gg = g_all[rows, :]
            sel = jnp.zeros((rg, LANES), jnp.bool_)
            for _ in range(MOBA_TOPK):
                mx = jnp.max(gg, axis=-1, keepdims=True)
                idx = jnp.min(jnp.where(gg == mx, lane, LANES), axis=-1, keepdims=True)
                hit = lane == idx
                sel = jnp.logical_or(sel, jnp.logical_and(hit, mx > NEG_INF / 2))
                gg = jnp.where(hit, -3e38, gg)
            mm = m_all[rows, :]
            m_tot = jnp.maximum(m_o, jnp.max(jnp.where(sel, mm, NEG_INF), axis=-1, keepdims=True))
            w = jnp.where(sel, jnp.exp(mm - m_tot), 0.0)
            w_o = jnp.exp(m_o - m_tot)
            l_tot = jnp.sum(w * l_all[rows, :], axis=-1, keepdims=True) + w_o * l_o
            o = w_o * o_o
            for nb in range(n_past):
                o = o + w[:, nb:nb + 1] * o_all[nb, rows, :]
            o = o / jnp.maximum(l_tot, 1e-30)
            for j in range(hpg):
                hh = g * hpg + j
                o_ref[:, hh * HEAD_DIM:(hh + 1) * HEAD_DIM] = o[j * tq:(j + 1) * tq].astype(o_ref.dtype)


def _moba_attend(p, b, t, pos0, rel_bias, cache, layer, page_table):
    h, grp = N_HEADS, MOBA_KV_HEADS
    gw = grp * HEAD_DIM
    paged = cache is not None
    assert pos0 % MOBA_BLOCK == 0
    if paged:
        assert t <= MOBA_BLOCK
        tq, nq = t, 1
        n_past = pos0 // MOBA_BLOCK
        own_keys = t
        nko = LANES
        ppb = MOBA_BLOCK // PAGE_SIZE
        npg = page_table.shape[1]
        pt = page_table.reshape(-1)
        cache4 = cache.reshape(cache.shape[0], cache.shape[1], PAGE_SIZE * 2 * grp, HEAD_DIM)

        bps = _pick(n_past, 4, 1)
        n_steps = n_past // bps

        def page_map(j):
            return lambda bi, qt, n, pt_: (layer, pt_[bi * npg + jnp.minimum(n, n_steps - 1) * bps * ppb + j], 0, 0)

        kv_specs = [pl.BlockSpec((None, None, PAGE_SIZE * 2 * grp, HEAD_DIM), page_map(j)) for j in range(bps * ppb)]
        kv_args = [cache4] * (bps * ppb)
        own_specs = [pl.BlockSpec((t, gw), lambda bi, qt, n, pt_: (bi, h * HEAD_DIM // gw)),
                     pl.BlockSpec((t, gw), lambda bi, qt, n, pt_: (bi, h * HEAD_DIM // gw + 1))]
        bias_past = jnp.stack([jnp.broadcast_to(jnp.repeat(rel_bias[N_BUCKETS - 1], tq)[:, None], (h * tq, MOBA_BLOCK)),
                               _toeplitz_bias(rel_bias, tq, MOBA_BLOCK, MOBA_BLOCK)])
        bias_own = _toeplitz_bias(rel_bias, tq, nko, 0)[None]
        bp_spec = pl.BlockSpec((2, h * tq, MOBA_BLOCK), lambda bi, qt, n, pt_: (0, 0, 0))

        def bo_map(bi, qt, n, pt_):
            return (0, 0, 0)
    else:
        assert t % MOBA_BLOCK == 0
        tq = 128
        nq = t // tq
        nblk = t // MOBA_BLOCK
        n_past = nblk - 1
        own_keys = MOBA_BLOCK
        nko = MOBA_BLOCK
        pt = jnp.zeros((1,), jnp.int32)
        qpb = MOBA_BLOCK // tq

        def past_map(col):
            return lambda bi, qt, n, pt_: (bi * nblk + jnp.minimum(n, jnp.maximum(qt // qpb - 1, 0)), col)

        kcol = h * HEAD_DIM // gw
        kv_specs = [pl.BlockSpec((MOBA_BLOCK, gw), past_map(kcol)), pl.BlockSpec((MOBA_BLOCK, gw), past_map(kcol + 1))]
        kv_args = [p, p]
        own_specs = [pl.BlockSpec((MOBA_BLOCK, gw), lambda bi, qt, n, pt_: (bi * nblk + qt // qpb, kcol)),
                     pl.BlockSpec((MOBA_BLOCK, gw), lambda bi, qt, n, pt_: (bi * nblk + qt // qpb, kcol + 1))]
        far = jnp.broadcast_to(jnp.repeat(rel_bias[N_BUCKETS - 1], tq)[:, None], (h * tq, MOBA_BLOCK))
        bias_past = jnp.stack([far] + [_toeplitz_bias(rel_bias, tq, MOBA_BLOCK, MOBA_BLOCK + r * tq) for r in range(qpb)])
        bias_own = jnp.stack([_toeplitz_bias(rel_bias, tq, MOBA_BLOCK, r * tq) for r in range(qpb)])

        bps, n_steps = 1, n_past
        bp_spec = pl.BlockSpec((1, h * tq, MOBA_BLOCK),
                               lambda bi, qt, n, pt_: (jnp.where(n == qt // qpb - 1, 1 + qt % qpb, 0), 0, 0))

        def bo_map(bi, qt, n, pt_):
            return (qt % qpb, 0, 0)

    rows = h * tq
    grid_spec = pltpu.PrefetchScalarGridSpec(
        num_scalar_prefetch=1,
        grid=(b, nq, n_steps + 1),
        in_specs=[pl.BlockSpec((tq, h * HEAD_DIM), lambda bi, qt, n, pt_: (bi * nq + qt, 0))]
        + kv_specs + own_specs
        + [bp_spec, pl.BlockSpec((None, rows, nko), bo_map)],
        out_specs=pl.BlockSpec((tq, h * HEAD_DIM), lambda bi, qt, n, pt_: (bi * nq + qt, 0)),
        scratch_shapes=[pltpu.VMEM((rows, LANES), F32), pltpu.VMEM((rows, LANES), F32), pltpu.VMEM((rows, LANES), F32),
                        pltpu.VMEM((max(n_past, 1), rows, HEAD_DIM), F32)],
    )
    assert n_past <= LANES
    return pl.pallas_call(
        functools.partial(_moba_kernel, tq=tq, pos0=pos0, n_past=n_past, paged=paged, own_keys=own_keys, bps=bps),
        grid_spec=grid_spec,
        out_shape=jax.ShapeDtypeStruct((b * t, h * HEAD_DIM), BF16),
        compiler_params=_cparams("parallel", "parallel", "arbitrary"),
    )(pt, p, *kv_args, p, p, bias_past, bias_own)


def _moba_layer(x, b, t, pos0, rel_bias, cache, layer, page_table, norm_mix, w_in, w_o):
    p = _mm(x, w_in.astype(BF16), norm_g=norm_mix)
    o = _moba_attend(p, b, t, pos0, rel_bias, cache, layer, page_table)
    kv = p[:, N_HEADS * HEAD_DIM:].reshape(b, t, 2, MOBA_KV_HEADS, HEAD_DIM)
    return _mm(o, w_o.astype(BF16), res=x), kv


NSA_HPG = N_HEADS // NSA_KV_HEADS
NSA_SCALE = HEAD_DIM ** -0.5
NSA_KV_COL0 = N_HEADS * HEAD_DIM // LANES
NSA_CACHE_ROWS = 4 * NSA_KV_HEADS
NSA_WIN_ROWS = 2 * NSA_KV_HEADS
NSA_PARTS = NSA_CMP_BLOCK // NSA_CMP_STRIDE
NSA_SEL_SHIFT = NSA_SEL_BLOCK.bit_length() - 1
assert 1 << NSA_SEL_SHIFT == NSA_SEL_BLOCK and NSA_PARTS == 2


def _dot3(a, b16):
    a1 = a.astype(BF16)
    r1 = a - a1.astype(F32)
    a2 = r1.astype(BF16)
    a3 = (r1 - a2.astype(F32)).astype(BF16)
    return _dot(a1, b16) + _dot(a2, b16) + _dot(a3, b16)


def _far_bias(rel_bias, tq, kb):
    return jnp.broadcast_to(jnp.repeat(rel_bias[N_BUCKETS - 1].astype(F32), tq)[:, None], (N_HEADS * tq, kb))


def _nsa_cmp_kernel(pt_ref, *refs, n_src, paged, cps):
    srcs = refs[:n_src]
    w_ref, o_ref = refs[n_src:]
    for c in range(2 * NSA_KV_HEADS):
        acc = jnp.zeros((cps, 2 * HEAD_DIM), F32)
        for s in range(NSA_CMP_STRIDE):
            if paged:
                xs = jnp.concatenate(
                    [r[pl.ds(s * NSA_CACHE_ROWS + c, PAGE_SIZE // NSA_CMP_STRIDE, stride=NSA_CMP_STRIDE * NSA_CACHE_ROWS), :]
                     for r in srcs], axis=0)
            else:
                xs = srcs[c][pl.ds(s, cps, stride=NSA_CMP_STRIDE), :]
            acc = acc + _dot(xs.astype(BF16), w_ref[c // NSA_KV_HEADS, s])
        o_ref[c] = acc


def _nsa_compress(p, b, t, pos0, pool, layer, page_table, w_cmp):
    ncomb = 2 * NSA_KV_HEADS
    wc = w_cmp.reshape(2, NSA_PARTS, NSA_CMP_STRIDE, HEAD_DIM, HEAD_DIM).transpose(0, 2, 3, 1, 4)
    wc = wc.reshape(2, NSA_CMP_STRIDE, HEAD_DIM, NSA_PARTS * HEAD_DIM).astype(BF16)
    w_spec = pl.BlockSpec(wc.shape, lambda bi, st, pt_: (0, 0, 0, 0))
    if pool is None:
        assert t % NSA_CMP_STRIDE == 0
        n_chunk = t // NSA_CMP_STRIDE
        cps, nst = n_chunk, 1
        pt = jnp.zeros((1,), jnp.int32)
        srcs = [p] * ncomb
        src_specs = [pl.BlockSpec((t, HEAD_DIM), (lambda c: lambda bi, st, pt_: (bi, NSA_KV_COL0 + c))(c))
                     for c in range(ncomb)]
    else:
        assert pos0 % NSA_CMP_STRIDE == 0 and t < NSA_CMP_STRIDE
        npg = page_table.shape[1]
        npp = _pick(npg, 16, 1)
        nst = npg // npp
        cps = npp * (PAGE_SIZE // NSA_CMP_STRIDE)
        n_chunk = nst * cps
        pt = page_table.reshape(-1)
        pool4 = pool.reshape(pool.shape[0], pool.shape[1], PAGE_SIZE * NSA_CACHE_ROWS, HEAD_DIM)
        srcs = [pool4] * npp
        src_specs = [pl.BlockSpec((None, None, PAGE_SIZE * NSA_CACHE_ROWS, HEAD_DIM),
                                  (lambda j: lambda bi, st, pt_: (layer, pt_[bi * npg + st * npp + j], 0, 0))(j))
                     for j in range(npp)]
    grid_spec = pltpu.PrefetchScalarGridSpec(
        num_scalar_prefetch=1,
        grid=(b, nst),
        in_specs=src_specs + [w_spec],
        out_specs=pl.BlockSpec((None, ncomb, cps, NSA_PARTS * HEAD_DIM), lambda bi, st, pt_: (bi, 0, st, 0)),
    )
    return pl.pallas_call(
        functools.partial(_nsa_cmp_kernel, n_src=len(srcs), paged=pool is not None, cps=cps),
        grid_spec=grid_spec,
        out_shape=jax.ShapeDtypeStruct((b, ncomb, n_chunk, NSA_PARTS * HEAD_DIM), F32),
        compiler_params=_cparams("parallel", "parallel"),
    )(pt, *srcs, wc)


def _topk_mask(score, lane, k):
    sel = jnp.zeros(score.shape, jnp.bool_)
    width = score.shape[-1]
    for _ in range(k):
        mx = jnp.max(score, axis=-1, keepdims=True)
        idx = jnp.min(jnp.where(score == mx, lane, width), axis=-1, keepdims=True)
        hit = lane == idx
        sel = jnp.logical_or(sel, jnp.logical_and(hit, mx > NEG_INF / 2))
        score = jnp.where(hit, -3e38, score)
    return sel


def _nsa_select_kernel(q_ref, pr_ref, bc_ref, bias_ref, imap_ref, oc_ref, sel_ref, *, tq, pos0, n_cmp, n_top):
    qt = pl.program_id(1)
    grp, hpg = NSA_KV_HEADS, NSA_HPG
    ncp, nsp = pr_ref.shape[1], imap_ref.shape[1]
    rg = hpg * tq
    t0 = pos0 + qt * tq
    t_row = t0 + lax.rem(lax.broadcasted_iota(jnp.int32, (rg, ncp), 0), tq)
    cidx = lax.broadcasted_iota(jnp.int32, (rg, ncp), 1)
    mask = jnp.logical_and(cidx * NSA_CMP_STRIDE + (NSA_CMP_BLOCK - 1) <= t_row, cidx < n_cmp)
    ti = t0 + lax.broadcasted_iota(jnp.int32, (tq, nsp), 0)
    blk = lax.broadcasted_iota(jnp.int32, (tq, nsp), 1)
    tb = jnp.right_shift(ti, NSA_SEL_SHIFT)
    forced = jnp.logical_or(blk == 0, jnp.logical_or(blk == tb, blk == tb - 1))
    for g in range(grp):
        pk, pv = pr_ref[g], pr_ref[grp + g]
        kc = bc_ref[0:1] + pk[:, 0:HEAD_DIM] + pltpu.roll(pk[:, HEAD_DIM:2 * HEAD_DIM], ncp - 1, axis=0)
        vc = bc_ref[1:2] + pv[:, 0:HEAD_DIM] + pltpu.roll(pv[:, HEAD_DIM:2 * HEAD_DIM], ncp - 1, axis=0)
        qg = (_group_rows(q_ref, g, hpg) * NSA_SCALE).astype(BF16)
        lg = _dot_nt(qg, kc.astype(BF16)) + bias_ref[g * rg:(g + 1) * rg, :]
        z = jnp.where(mask, lg, NEG_INF)
        z = z - jnp.max(z, axis=-1, keepdims=True)
        e = jnp.where(mask, jnp.exp(z), 0.0)
        pc = e / jnp.maximum(jnp.sum(e, axis=-1, keepdims=True), 1e-30)
        oc = _dot(pc.astype(BF16), vc.astype(BF16))
        psum = pc[0:tq]
        for j in range(hpg):
            hh = g * hpg + j
            oc_ref[:, hh * HEAD_DIM:(hh + 1) * HEAD_DIM] = oc[j * tq:(j + 1) * tq]
            if j:
                psum = psum + pc[j * tq:(j + 1) * tq]
        imp = _dot3(psum, imap_ref[...])
        score = jnp.where(blk <= tb, jnp.where(forced, NSA_FORCE, imp), NEG_INF)
        sel_ref[:, g * nsp:(g + 1) * nsp] = _topk_mask(score, blk, n_top).astype(F32)


def _nsa_select(p, proj, b, t, pos0, rel_bias, b_cmp, tq):
    h, grp = N_HEADS, NSA_KV_HEADS
    nq = t // tq
    ncp = proj.shape[2]
    seq_len = pos0 + t
    n_cmp = seq_len // NSA_CMP_STRIDE - NSA_PARTS + 1
    n_sel = -(-seq_len // NSA_SEL_BLOCK)
    nsp = -(-n_sel // LANES) * LANES
    assert n_cmp <= ncp
    c0 = np.arange(ncp)[:, None] * NSA_CMP_STRIDE
    s0 = np.arange(nsp)[None, :] * NSA_SEL_BLOCK
    inter = np.minimum(c0 + NSA_CMP_BLOCK, s0 + NSA_SEL_BLOCK) - np.maximum(c0, s0)
    imap = np.maximum(inter, 0) / NSA_CMP_BLOCK
    imap = np.where((np.arange(ncp)[:, None] < n_cmp) & (np.arange(nsp)[None, :] < n_sel), imap, 0.0)
    imap = jnp.asarray(imap, F32).astype(BF16)
    tpos = pos0 + jnp.arange(nq * tq, dtype=jnp.int32).reshape(nq, tq)
    d = tpos[:, :, None] - (jnp.arange(ncp, dtype=jnp.int32) * NSA_CMP_STRIDE + (NSA_CMP_BLOCK - 1))[None, None, :]
    bias = rel_bias[rel_bucket(d)].astype(F32).transpose(0, 3, 1, 2).reshape(nq, h * tq, ncp)
    return pl.pallas_call(
        functools.partial(_nsa_select_kernel, tq=tq, pos0=pos0, n_cmp=n_cmp, n_top=min(NSA_N_SEL, n_sel)),
        grid=(b, nq),
        in_specs=[pl.BlockSpec((tq, h * HEAD_DIM), lambda bi, qt: (bi * nq + qt, 0)),
                  pl.BlockSpec((None, 2 * grp, ncp, NSA_PARTS * HEAD_DIM), lambda bi, qt: (bi, 0, 0, 0)),
                  pl.BlockSpec((2, HEAD_DIM), lambda bi, qt: (0, 0)),
                  pl.BlockSpec((None, h * tq, ncp), lambda bi, qt: (qt, 0, 0)),
                  pl.BlockSpec((ncp, nsp), lambda bi, qt: (0, 0))],
        out_specs=[pl.BlockSpec((tq, h * HEAD_DIM), lambda bi, qt: (bi * nq + qt, 0)),
                   pl.BlockSpec((tq, grp * nsp), lambda bi, qt: (bi * nq + qt, 0))],
        out_shape=[jax.ShapeDtypeStruct((b * t, h * HEAD_DIM), F32),
                   jax.ShapeDtypeStruct((b * t, grp * nsp), F32)],
        compiler_params=_cparams("parallel", "parallel"),
    )(p, proj, b_cmp, bias, imap)


def _softmax_rows(s, mask, m_ref, l_ref, rows):
    z = jnp.where(mask, s, NEG_INF)
    m_old = m_ref[rows, :]
    m_new = jnp.maximum(m_old, jnp.max(z, axis=-1, keepdims=True))
    p = jnp.where(mask, jnp.exp(z - m_new), 0.0)
    alpha = jnp.exp(m_old - m_new)
    l_ref[rows, :] = alpha * l_ref[rows, :] + jnp.sum(p, axis=-1, keepdims=True)
    m_ref[rows, :] = m_new
    return alpha, p


def _fa_tile(qt, kt, *, tq, kb, n_main, pos0, k_base, rel_window):
    t0 = pos0 + qt * tq
    tile = (t0 - k_base) // kb - (n_main - 1) + kt if rel_window else kt
    k0 = k_base + tile * kb
    active = jnp.logical_and(tile >= 0, k0 <= t0 + tq - 1)
    return t0, tile, k0, active


def _nsa_fa_kernel(pt_ref, *refs, kind, kv_mode, tq, kb, n_main, pos0, k_base, rel_window, tail, ts, win_lo, nsp, tail_blk):
    grp, hpg = NSA_KV_HEADS, NSA_HPG
    it = iter(refs)
    q_ref = next(it)
    if kind == 'sel':
        sel_ref, e_ref = next(it), next(it)
    n_kv = {'cols': 2 * grp, 'paged': kb // PAGE_SIZE, 'buf': 1}[kv_mode]
    kv = [next(it) for _ in range(n_kv)]
    tl = [next(it) for _ in range(2 * grp)] if tail else None
    bias_ref = next(it)
    bt_ref = next(it) if tail else None
    o_ref, m_ref, l_ref, acc_ref = next(it), next(it), next(it), next(it)
    qt, kt = pl.program_id(1), pl.program_id(2)
    t0, _, k0, active = _fa_tile(qt, kt, tq=tq, kb=kb, n_main=n_main, pos0=pos0, k_base=k_base, rel_window=rel_window)
    rg = hpg * tq

    @pl.when(kt == 0)
    def _():
        m_ref[...] = jnp.full_like(m_ref, NEG_INF)
        l_ref[...] = jnp.zeros_like(l_ref)
        acc_ref[...] = jnp.zeros_like(acc_ref)

    def load(g, which):
        if kv_mode == 'cols':
            return kv[which * grp + g][...]
        if kv_mode == 'paged':
            off = (2 + which) * grp + g
            return jnp.concatenate([r[pl.ds(off, PAGE_SIZE, stride=NSA_CACHE_ROWS), :] for r in kv], axis=0)
        return kv[0][pl.ds(which * grp + g, kb, stride=NSA_WIN_ROWS), :]

    def base_mask(first_key, nk):
        ti = t0 + lax.rem(lax.broadcasted_iota(jnp.int32, (rg, nk), 0), tq)
        kj = first_key + lax.broadcasted_iota(jnp.int32, (rg, nk), 1)
        m = kj <= ti
        if kind == 'win':
            m = jnp.logical_and(m, jnp.logical_and(ti - kj <= NSA_WINDOW, kj >= win_lo))
        return m

    def attend(g, kg, vg, bias, m):
        rows = slice(g * rg, (g + 1) * rg)
        qg = (_group_rows(q_ref, g, hpg) * NSA_SCALE).astype(BF16)
        s = _dot_nt(qg, kg.astype(BF16)) + bias
        alpha, p = _softmax_rows(s, m, m_ref, l_ref, rows)
        acc_ref[rows, :] = alpha * acc_ref[rows, :] + _dot(p.astype(BF16), vg.astype(BF16))

    @pl.when(active)
    def _():
        base = base_mask(k0, kb)
        for g in range(grp):
            m = base
            if kind == 'sel':
                ex = _dot(sel_ref[:, g * nsp:(g + 1) * nsp].astype(BF16), e_ref[...])
                m = jnp.logical_and(m, jnp.concatenate([ex] * hpg, axis=0) > 0.5)
            attend(g, load(g, 0), load(g, 1), bias_ref[g * rg:(g + 1) * rg, :], m)

    @pl.when(kt == n_main - 1)
    def _():
        if tail:
            base = base_mask(pos0, LANES)
            zpad = jnp.zeros((LANES - ts, HEAD_DIM), F32)
            for g in range(grp):
                m = base
                if kind == 'sel':
                    col = sel_ref[:, g * nsp + tail_blk:g * nsp + tail_blk + 1]
                    m = jnp.logical_and(m, jnp.concatenate([col] * hpg, axis=0) > 0.5)
                kg = jnp.concatenate([tl[g][...], zpad], axis=0)
                vg = jnp.concatenate([tl[grp + g][...], zpad], axis=0)
                attend(g, kg, vg, bt_ref[g * rg:(g + 1) * rg, :], m)
        o = acc_ref[...] / jnp.maximum(l_ref[...], 1e-30)
        for hh in range(N_HEADS):
            o_ref[:, hh * HEAD_DIM:(hh + 1) * HEAD_DIM] = o[hh * tq:(hh + 1) * tq]


def _nsa_fa(kind, p, sel, b, t, pos0, rel_bias, pool, layer, page_table, win_buf, tq):
    h, grp = N_HEADS, NSA_KV_HEADS
    nq = t // tq
    kinds = (2, 3) if kind == 'sel' else (4, 5)
    sample = page_table is not None
    seq_len = pos0 + t
    nsp = -(-(-(-seq_len // NSA_SEL_BLOCK)) // LANES) * LANES
    if not sample:
        kb = 2 * tq
        assert t % kb == 0 and NSA_WINDOW % kb == 0
        n_main = t // kb if kind == 'sel' else NSA_WINDOW // kb + 1
        cfg = dict(tq=tq, kb=kb, n_main=n_main, pos0=0, k_base=0, rel_window=kind == 'win')
        cs = (0, tq, 2 * tq, 3 * tq)
        kv_mode, tail, tail_blk, win_lo = 'cols', False, 0, 0
        ntile = t // kb
        pt = jnp.zeros((1,), jnp.int32)
    else:
        assert t <= NSA_SEL_BLOCK and pos0 % NSA_SEL_BLOCK == 0
        tail, tail_blk = True, pos0 // NSA_SEL_BLOCK
        pt = page_table.reshape(-1)
        npg = page_table.shape[1]
        if kind == 'sel':
            kb = _pick(pos0, 512, PAGE_SIZE)
            cfg = dict(tq=tq, kb=kb, n_main=pos0 // kb, pos0=pos0, k_base=0, rel_window=False)
            kv_mode, win_lo = 'paged', 0
        else:
            kb = win_buf.shape[1]
            cfg = dict(tq=tq, kb=kb, n_main=1, pos0=pos0, k_base=pos0 - kb, rel_window=False)
            kv_mode, win_lo = 'buf', pos0 - kb
        cs = (kb,)
        ntile = cfg['n_main']
    n_main = cfg['n_main']

    def tile_c(qt, kt):
        t0, tile, _, _ = _fa_tile(qt, kt, **cfg)
        return jnp.clip(tile, 0, jnp.minimum((t0 + tq - 1 - cfg['k_base']) // kb, ntile - 1))

    def bias_idx(bi, qt, kt, pt_):
        t0, _, k0, _ = _fa_tile(qt, kt, **cfg)
        idx = 0
        for i, c in enumerate(cs):
            idx = idx + jnp.where(t0 - k0 == c, i + 1, 0)
        return (idx, 0, 0)

    in_specs = [pl.BlockSpec((tq, h * HEAD_DIM), lambda bi, qt, kt, pt_: (bi * nq + qt, 0))]
    args = [p]
    if kind == 'sel':
        e = (np.arange(nsp)[None, :, None] == ((np.arange(ntile)[:, None, None] * kb + np.arange(kb)[None, None, :])
                                                 // NSA_SEL_BLOCK))
        in_specs += [pl.BlockSpec((tq, grp * nsp), lambda bi, qt, kt, pt_: (bi * nq + qt, 0)),
                     pl.BlockSpec((None, nsp, kb), lambda bi, qt, kt, pt_: (tile_c(qt, kt), 0, 0))]
        args += [sel, jnp.asarray(e, F32).astype(BF16)]
    if kv_mode == 'cols':
        for kd in kinds:
            for g in range(grp):
                col = NSA_KV_COL0 + kd * grp + g
                in_specs.append(pl.BlockSpec((kb, HEAD_DIM), (lambda col: lambda bi, qt, kt, pt_: (bi * ntile + tile_c(qt, kt), col))(col)))
                args.append(p)
    elif kv_mode == 'paged':
        ppt = kb // PAGE_SIZE
        pool4 = pool.reshape(pool.shape[0], pool.shape[1], PAGE_SIZE * NSA_CACHE_ROWS, HEAD_DIM)
        for j in range(ppt):
            in_specs.append(pl.BlockSpec((None, None, PAGE_SIZE * NSA_CACHE_ROWS, HEAD_DIM),
                                         (lambda j: lambda bi, qt, kt, pt_: (layer, pt_[bi * npg + kt * ppt + j], 0, 0))(j)))
            args.append(pool4)
    else:
        in_specs.append(pl.BlockSpec((None, kb * NSA_WIN_ROWS, HEAD_DIM), lambda bi, qt, kt, pt_: (bi, 0, 0)))
        args.append(win_buf.reshape(b, kb * NSA_WIN_ROWS, HEAD_DIM))
    if tail:
        for kd in kinds:
            for g in range(grp):
                col = NSA_KV_COL0 + kd * grp + g
                in_specs.append(pl.BlockSpec((t, HEAD_DIM), (lambda col: lambda bi, qt, kt, pt_: (bi, col))(col)))
                args.append(p)
    in_specs.append(pl.BlockSpec((None, h * tq, kb), bias_idx))
    args.append(jnp.stack([_far_bias(rel_bias, tq, kb)] + [_toeplitz_bias(rel_bias, tq, kb, c) for c in cs]))
    if tail:
        in_specs.append(pl.BlockSpec((h * tq, LANES), lambda bi, qt, kt, pt_: (0, 0)))
        args.append(_toeplitz_bias(rel_bias, tq, LANES, 0))
    grid_spec = pltpu.PrefetchScalarGridSpec(
        num_scalar_prefetch=1,
        grid=(b, nq, n_main),
        in_specs=in_specs,
        out_specs=pl.BlockSpec((tq, h * HEAD_DIM), lambda bi, qt, kt, pt_: (bi * nq + qt, 0)),
        scratch_shapes=[pltpu.VMEM((h * tq, 1), F32), pltpu.VMEM((h * tq, 1), F32), pltpu.VMEM((h * tq, HEAD_DIM), F32)],
    )
    return pl.pallas_call(
        functools.partial(_nsa_fa_kernel, kind=kind, kv_mode=kv_mode, tail=tail, ts=t, win_lo=win_lo, nsp=nsp,
                          tail_blk=tail_blk, **cfg),
        grid_spec=grid_spec,
        out_shape=jax.ShapeDtypeStruct((b * t, h * HEAD_DIM), F32),
        compiler_params=_cparams("parallel", "parallel", "arbitrary"),
    )(pt, *args)


def _nsa_merge_kernel(gl_ref, bg_ref, oc_ref, os_ref, ow_ref, o_ref):
    gates = _sigmoid(gl_ref[...] + bg_ref[...])
    for hh in range(N_HEADS):
        cols = slice(hh * HEAD_DIM, (hh + 1) * HEAD_DIM)
        o = (gates[:, 3 * hh:3 * hh + 1] * oc_ref[:, cols] + gates[:, 3 * hh + 1:3 * hh + 2] * os_ref[:, cols]
             + gates[:, 3 * hh + 2:3 * hh + 3] * ow_ref[:, cols])
        o_ref[:, cols] = o.astype(o_ref.dtype)


def _nsa_merge(p, b_gate, o_c, o_s, o_w):
    m = p.shape[0]
    d = N_HEADS * HEAD_DIM
    tm = _pick(m, 256, SUBLANES_BF16)
    gcol = (d + 6 * NSA_KV_HEADS * HEAD_DIM) // LANES
    o_spec = pl.BlockSpec((tm, d), lambda i: (i, 0))
    return pl.pallas_call(
        _nsa_merge_kernel,
        grid=(m // tm,),
        in_specs=[pl.BlockSpec((tm, LANES), lambda i: (i, gcol)), pl.BlockSpec((1, LANES), lambda i: (0, 0)),
                  o_spec, o_spec, o_spec],
        out_specs=o_spec,
        out_shape=jax.ShapeDtypeStruct((m, d), BF16),
        compiler_params=_cparams("parallel"),
    )(p, _pad_cols(b_gate.reshape(1, -1)), o_c, o_s, o_w)


def _nsa_layer(x, b, t, pos0, rel_bias, pool, layer, win_buf, page_table, norm_mix, w_in, b_gate, w_cmp, b_cmp, w_o):
    nq, nkv = N_HEADS * HEAD_DIM, 6 * NSA_KV_HEADS * HEAD_DIM
    p = _mm(x, _pad_cols(w_in, 2 * LANES).astype(BF16), norm_g=norm_mix)
    tq = t if pool is not None else 128
    proj = _nsa_compress(p, b, t, pos0, pool, layer, page_table, w_cmp)
    o_c, sel = _nsa_select(p, proj, b, t, pos0, rel_bias, b_cmp, tq)
    o_s = _nsa_fa('sel', p, sel, b, t, pos0, rel_bias, pool, layer, page_table, win_buf, tq)
    o_w = _nsa_fa('win', p, sel, b, t, pos0, rel_bias, pool, layer, page_table, win_buf, tq)
    o = _nsa_merge(p, b_gate, o_c, o_s, o_w)
    kv = p[:, nq:nq + nkv].reshape(b, t, 6, NSA_KV_HEADS, HEAD_DIM)
    if pool is None:
        win_out = kv[:, t - min(NSA_WINDOW, t):, 4:]
    else:
        win_out = jnp.concatenate([win_buf, kv[:, :, 4:]], axis=1)[:, -win_buf.shape[1]:]
    return _mm(o, w_o.astype(BF16), res=x), kv[:, :, :4], win_out


def _ffn_layer(x, b, t, state, norm_ffn, w_in, conv_w, conv_b, w_out):
    gu = _mm(x, w_in.astype(BF16), norm_g=norm_ffn)
    a = _ffn_gate(gu, conv_w, conv_b, t, state)
    y = _mm(a, w_out.astype(BF16), res=x)
    g_tail = gu.reshape(b, t, 2 * D_FF)[:, max(t - (CONV_W - 1), 0):, :D_FF]
    if state is None:
        prev = jnp.zeros((b, CONV_W - 1, D_FF), F32)
    else:
        prev = state
    new_state = jnp.concatenate([prev, g_tail], axis=1)[:, -(CONV_W - 1):]
    return y, new_state


def kernel(x_prompt, x_sample, state_gla, cache_nsa_kv, cache_nsa_win, cache_mla, cache_moba_kv, state_ffn_conv, page_table, rel_bias, norm_mix, norm_ffn, norm_final, ffn_w_in, ffn_conv_w, ffn_conv_b, ffn_w_out, gla_w_in, gla_b_r, gla_w_a1, gla_w_a2, gla_b_a, gla_norm, gla_w_o, nsa_w_in, nsa_b_gate, nsa_w_cmp, nsa_b_cmp, nsa_w_o, mla_w_down, mla_q_norm, mla_w_uq, mla_kv_norm, mla_w_uk, mla_w_uv, mla_w_o, moba_w_in, moba_w_o):
    bp, tp, d = x_prompt.shape
    bs, ts, _ = x_sample.shape
    past = page_table.shape[1] * PAGE_SIZE
    pos_p = jnp.arange(tp, dtype=jnp.int32)
    pos_s = past + jnp.arange(ts, dtype=jnp.int32)
    xp = x_prompt.reshape(bp * tp, d)
    xs = x_sample.reshape(bs * ts, d)
    outs = {k: [] for k in ('gla_p', 'gla_s', 'nkv_p', 'nkv_s', 'nwin_p', 'nwin_s', 'mla_p', 'mla_s', 'mkv_p', 'mkv_s',
                            'conv_p', 'conv_s')}
    for i in range(DEPTH):
        m, j = i % N_MIXERS, i // N_MIXERS
        if m == 0:
            w = (norm_mix[i], gla_w_in[j], gla_b_r[j], gla_w_a1[j], gla_w_a2[j], gla_b_a[j], gla_norm[j], gla_w_o[j])
            xp, st = _gla_layer(xp, bp, tp, None, *w)
            outs['gla_p'].append(st)
            xs, st = _gla_layer(xs, bs, ts, state_gla[j], *w)
            outs['gla_s'].append(st)
        elif m == 1:
            w = (norm_mix[i], nsa_w_in[j], nsa_b_gate[j], nsa_w_cmp[j], nsa_b_cmp[j], nsa_w_o[j])
            xp, kv, win = _nsa_layer(xp, bp, tp, 0, rel_bias, None, j, None, None, *w)
            outs['nkv_p'].append(kv)
            outs['nwin_p'].append(win)
            xs, kv, win = _nsa_layer(xs, bs, ts, past, rel_bias, cache_nsa_kv, j, cache_nsa_win[j], page_table, *w)
            outs['nkv_s'].append(kv)
            outs['nwin_s'].append(win)
        elif m == 2:
            w = (norm_mix[i], mla_w_down[j], mla_q_norm[j], mla_w_uq[j], mla_kv_norm[j], mla_w_uk[j], mla_w_uv[j],
                 mla_w_o[j])
            xp, lat = _mla_layer(xp, bp, tp, pos_p, None, j, None, *w)
            outs['mla_p'].append(lat)
            xs, lat = _mla_layer(xs, bs, ts, pos_s, cache_mla, j, page_table, *w)
            outs['mla_s'].append(lat)
        else:
            w = (norm_mix[i], moba_w_in[j], moba_w_o[j])
            xp, kv = _moba_layer(xp, bp, tp, 0, rel_bias, None, j, None, *w)
            outs['mkv_p'].append(kv)
            xs, kv = _moba_layer(xs, bs, ts, past, rel_bias, cache_moba_kv, j, page_table, *w)
            outs['mkv_s'].append(kv)
        wf = (norm_ffn[i], ffn_w_in[i], ffn_conv_w[i], ffn_conv_b[i], ffn_w_out[i])
        xp, cst = _ffn_layer(xp, bp, tp, None, *wf)
        outs['conv_p'].append(cst)
        xs, cst = _ffn_layer(xs, bs, ts, state_ffn_conv[i], *wf)
        outs['conv_s'].append(cst)
    y_prompt = _rms(xp, norm_final).reshape(bp, tp, d)
    y_sample = _rms(xs, norm_final).reshape(bs, ts, d)
    return (y_prompt, y_sample) + tuple(jnp.stack(outs[k]) for k in (
        'gla_p', 'gla_s', 'nkv_p', 'nkv_s', 'nwin_p', 'nwin_s', 'mla_p', 'mla_s', 'mkv_p', 'mkv_s', 'conv_p', 'conv_s'))
```

```python
import functools
import math

import numpy as np
import jax
import jax.numpy as jnp
from jax import lax
from jax.experimental import pallas as pl
from jax.experimental.pallas import tpu as pltpu

F32 = jnp.float32
BF16 = jnp.bfloat16

D_MODEL = 2048
DEPTH = 4
PAGE_SIZE = 128
N_MIXERS = 4
N_HEADS = 16
HEAD_DIM = D_MODEL // N_HEADS
N_BUCKETS = 32
MAX_DISTANCE = 128
RMS_EPS = 1e-6
NEG_INF = -1e30

GLA_HEADS = 4
GLA_DK = D_MODEL // 2 // GLA_HEADS
GLA_DV = D_MODEL // GLA_HEADS
GLA_GATE_RANK = 16
GLA_TAU = 16.0
GLA_CHUNK = 64

NSA_KV_HEADS = 2
NSA_CMP_BLOCK = 32
NSA_CMP_STRIDE = 16
NSA_SEL_BLOCK = 64
NSA_N_SEL = 16
NSA_WINDOW = 512
NSA_QBLOCK = 64
NSA_FORCE = 1e9

MLA_Q_RANK = 512
MLA_KV_RANK = 512
MLA_NOPE = 128
MLA_ROPE = 64
MLA_V = 128
ROPE_THETA = 10000.0

MOBA_KV_HEADS = 4
MOBA_BLOCK = 256
MOBA_TOPK = 3

D_FF = 5632
CONV_W = 3

V7X_VMEM_BYTES = 64 * 1024 * 1024
VMEM_LIMIT_BYTES = V7X_VMEM_BYTES * 3 // 4
LANES = 128
SUBLANES_BF16 = 16


def _cparams(*sem):
    return pltpu.CompilerParams(dimension_semantics=sem, vmem_limit_bytes=VMEM_LIMIT_BYTES)


def _pick(n, cap, mult):
    best = None
    for d in range(mult, min(n, cap) + 1, mult):
        if n % d == 0:
            best = d
    assert best is not None, (n, cap, mult)
    return best


def _pad_cols(w, mult=LANES):
    n = w.shape[-1]
    pad = (-n) % mult
    if pad:
        w = jnp.pad(w, [(0, 0)] * (w.ndim - 1) + [(0, pad)])
    return w


def _dot(a, b):
    return jnp.dot(a, b, preferred_element_type=F32)


def _dot_nt(a, b):
    return lax.dot_general(a, b, (((1,), (1,)), ((), ())), preferred_element_type=F32)


def _dot_tn(a, b):
    return lax.dot_general(a, b, (((0,), (0,)), ((), ())), preferred_element_type=F32)


def _sigmoid(x):
    return 1.0 / (1.0 + jnp.exp(-x))


def _mm_kernel(*refs, has_norm, has_res):
    it = iter(refs)
    x_ref, w_ref = next(it), next(it)
    g_ref = next(it) if has_norm else None
    r_ref = next(it) if has_res else None
    o_ref = next(it)
    if has_norm:
        h_ref = next(it)

        @pl.when(pl.program_id(1) == 0)
        def _():
            xf = x_ref[...].astype(F32)
            y = xf * lax.rsqrt(jnp.mean(xf * xf, axis=-1, keepdims=True) + RMS_EPS)
            h_ref[...] = (y * g_ref[...]).astype(BF16)

        lhs = h_ref[...]
    else:
        lhs = x_ref[...].astype(BF16)
    acc = _dot(lhs, w_ref[...])
    if has_res:
        acc = acc + r_ref[...]
    o_ref[...] = acc.astype(o_ref.dtype)


def _mm(x, w, *, x_col=0, norm_g=None, res=None, out_dtype=F32, tm_cap=1024, tn_cap=1024):
    m = x.shape[0]
    k, n = w.shape
    assert n % LANES == 0
    tm = _pick(m, tm_cap, SUBLANES_BF16)
    tn = n if n <= 1536 else _pick(n, tn_cap, LANES)
    if k > 4096:
        tm = _pick(m, 512, SUBLANES_BF16)
    in_specs = [pl.BlockSpec((tm, k), lambda i, j: (i, x_col)),
                pl.BlockSpec((k, tn), lambda i, j: (0, j))]
    args = [x, w]
    scratch = []
    if norm_g is not None:
        in_specs.append(pl.BlockSpec((1, k), lambda i, j: (0, 0)))
        args.append(norm_g.reshape(1, k).astype(F32))
        scratch.append(pltpu.VMEM((tm, k), BF16))
    if res is not None:
        in_specs.append(pl.BlockSpec((tm, tn), lambda i, j: (i, j)))
        args.append(res)
    return pl.pallas_call(
        functools.partial(_mm_kernel, has_norm=norm_g is not None, has_res=res is not None),
        grid=(m // tm, n // tn),
        in_specs=in_specs,
        out_specs=pl.BlockSpec((tm, tn), lambda i, j: (i, j)),
        out_shape=jax.ShapeDtypeStruct((m, n), out_dtype),
        scratch_shapes=scratch,
        compiler_params=_cparams("parallel", "arbitrary"),
    )(*args)


def _rms_kernel(x_ref, g_ref, o_ref):
    xf = x_ref[...]
    y = xf * lax.rsqrt(jnp.mean(xf * xf, axis=-1, keepdims=True) + RMS_EPS)
    o_ref[...] = y * g_ref[...]


def _rms(x, g):
    m, d = x.shape
    tm = _pick(m, 512, 8)
    return pl.pallas_call(
        _rms_kernel,
        grid=(m // tm,),
        in_specs=[pl.BlockSpec((tm, d), lambda i: (i, 0)), pl.BlockSpec((1, d), lambda i: (0, 0))],
        out_specs=pl.BlockSpec((tm, d), lambda i: (i, 0)),
        out_shape=jax.ShapeDtypeStruct((m, d), F32),
        compiler_params=_cparams("parallel"),
    )(x, g.reshape(1, d))


def _gate_kernel(*refs, seq_len, tr, has_state):
    if has_state:
        g_ref, u_ref, gp_ref, e1_ref, e2_ref, cw_ref, cb_ref, o_ref = refs
    else:
        g_ref, u_ref, gp_ref, cw_ref, cb_ref, o_ref = refs
    g = g_ref[...]
    gp = gp_ref[...]
    rl = lax.broadcasted_iota(jnp.int32, (tr, 1), 0)
    g1 = jnp.where(rl == 0, gp[7:8], pltpu.roll(g, 1, axis=0))
    g2 = jnp.where(rl == 0, gp[6:7], jnp.where(rl == 1, gp[7:8], pltpu.roll(g, 2, axis=0)))
    if seq_len >= tr:
        pos = lax.rem(pl.program_id(0) * tr, seq_len) + rl
    else:
        pos = lax.rem(rl, seq_len)
    if has_state:
        e1, e2 = e1_ref[...], e2_ref[...]
    else:
        e1 = e2 = jnp.zeros_like(g)
    g1 = jnp.where(pos >= 1, g1, e1)
    g2 = jnp.where(pos >= 2, g2, e2)
    cw = cw_ref[...]
    gc = cb_ref[...] + g2 * cw[0:1] + g1 * cw[1:2] + g * cw[2:3]
    o_ref[...] = (gc * _sigmoid(gc) * u_ref[...]).astype(o_ref.dtype)


def _ffn_gate(gu, conv_w, conv_b, seq_len, state):
    m = gu.shape[0]
    tc = _pick(D_FF, 1024, LANES)
    ncb = D_FF // tc
    tr = _pick(seq_len, 512, 8) if seq_len >= 16 else _pick(m, 512, 8 * seq_len // math.gcd(8, seq_len))
    if seq_len < tr:
        assert tr % seq_len == 0 and seq_len >= CONV_W - 1
    else:
        assert seq_len % tr == 0
    in_specs = [pl.BlockSpec((tr, tc), lambda i, j: (i, j)),
                pl.BlockSpec((tr, tc), lambda i, j: (i, j + ncb)),
                pl.BlockSpec((8, tc), lambda i, j: (jnp.maximum(i * (tr // 8) - 1, 0), j))]
    args = [gu, gu, gu]
    if state is not None:
        b = state.shape[0]
        z = jnp.zeros((b, seq_len, D_FF), F32)
        e1 = z.at[:, 0].set(state[:, 1]).reshape(m, D_FF)
        e2 = z.at[:, 0].set(state[:, 0]).at[:, 1].set(state[:, 1]).reshape(m, D_FF)
        in_specs += [pl.BlockSpec((tr, tc), lambda i, j: (i, j))] * 2
        args += [e1, e2]
    in_specs += [pl.BlockSpec((8, tc), lambda i, j: (0, j)), pl.BlockSpec((1, tc), lambda i, j: (0, j))]
    args += [jnp.pad(conv_w, ((0, 8 - CONV_W), (0, 0))), conv_b.reshape(1, D_FF)]
    return pl.pallas_call(
        functools.partial(_gate_kernel, seq_len=seq_len, tr=tr, has_state=state is not None),
        grid=(m // tr, ncb),
        in_specs=in_specs,
        out_specs=pl.BlockSpec((tr, tc), lambda i, j: (i, j)),
        out_shape=jax.ShapeDtypeStruct((m, D_FF), BF16),
        compiler_params=_cparams("parallel", "parallel"),
    )(*args)


def _cumsum_rows(x):
    c = x.shape[0]
    row = lax.broadcasted_iota(jnp.int32, x.shape, 0)
    s = 1
    while s < c:
        x = x + jnp.where(row >= s, pltpu.roll(x, s, axis=0), 0.0)
        s *= 2
    return x


def _gla_kernel(*refs, c, has_state):
    it = iter(refs)
    q_ref, k_ref, v_ref, r_ref, ga_ref, wa2_ref, ba_ref, ng_ref, br_ref = (next(it) for _ in range(9))
    s0_ref = next(it) if has_state else None
    o_ref, sout_ref, st_ref = next(it), next(it), next(it)
    ci = pl.program_id(2)

    @pl.when(ci == 0)
    def _():
        if has_state:
            st_ref[...] = s0_ref[...].T
        else:
            st_ref[...] = jnp.zeros_like(st_ref)

    q = q_ref[...] * (GLA_DK ** -0.5)
    k = k_ref[...]
    v16 = v_ref[...].astype(BF16)
    lr = _dot(ga_ref[...].astype(BF16), wa2_ref[...]) + ba_ref[...]
    log_a = (jnp.minimum(lr, 0.0) - jnp.log(1.0 + jnp.exp(-jnp.abs(lr)))) / GLA_TAU
    cum = _cumsum_rows(log_a)
    q_dec = (q * jnp.exp(cum)).astype(BF16)
    k_inv = (k * jnp.exp(-cum)).astype(BF16)
    att = _dot_nt(q_dec, k_inv)
    ri = lax.broadcasted_iota(jnp.int32, (c, c), 0)
    cj = lax.broadcasted_iota(jnp.int32, (c, c), 1)
    att = jnp.where(ri >= cj, att, 0.0)
    st = st_ref[...]
    o = _dot(att.astype(BF16), v16) + _dot_nt(q_dec, st.astype(BF16))
    last = cum[c - 1:c]
    k_dec = (k * jnp.exp(last - cum)).astype(BF16)
    st_new = st * jnp.exp(last) + _dot_tn(v16, k_dec)
    st_ref[...] = st_new
    on = o * lax.rsqrt(jnp.mean(o * o, axis=-1, keepdims=True) + RMS_EPS) * ng_ref[...]
    rr = r_ref[...] + br_ref[...]
    o_ref[...] = (on * (rr * _sigmoid(rr))).astype(o_ref.dtype)

    @pl.when(ci == pl.num_programs(2) - 1)
    def _():
        sout_ref[...] = st_new.T


def _gla_scan(p, b, t, w_a2p, b_a, norm_g, b_r, s0):
    c = GLA_CHUNK if t % GLA_CHUNK == 0 else t
    nch = t // c
    h = GLA_HEADS
    nk = h * GLA_DK
    kq, kv_ = GLA_DK, GLA_DV
    row = lambda bi, hi, ci: bi * nch + ci
    in_specs = [
        pl.BlockSpec((c, kq), lambda bi, hi, ci: (row(bi, hi, ci), hi)),
        pl.BlockSpec((c, kq), lambda bi, hi, ci: (row(bi, hi, ci), h + hi)),
        pl.BlockSpec((c, kv_), lambda bi, hi, ci: (row(bi, hi, ci), 2 * nk // kv_ + hi)),
        pl.BlockSpec((c, kv_), lambda bi, hi, ci: (row(bi, hi, ci), 2 * nk // kv_ + h + hi)),
        pl.BlockSpec((c, LANES), lambda bi, hi, ci: (row(bi, hi, ci), (2 * nk + 2 * h * kv_) // LANES)),
        pl.BlockSpec((LANES, kq), lambda bi, hi, ci: (0, hi)),
        pl.BlockSpec((1, kq), lambda bi, hi, ci: (0, hi)),
        pl.BlockSpec((1, kv_), lambda bi, hi, ci: (0, 0)),
        pl.BlockSpec((1, kv_), lambda bi, hi, ci: (0, hi)),
    ]
    args = [p, p, p, p, p, w_a2p, b_a.reshape(1, nk), norm_g.reshape(1, kv_), b_r.reshape(1, h * kv_)]
    if s0 is not None:
        in_specs.append(pl.BlockSpec((None, None, kq, kv_), lambda bi, hi, ci: (bi, hi, 0, 0)))
        args.append(s0)
    return pl.pallas_call(
        functools.partial(_gla_kernel, c=c, has_state=s0 is not None),
        grid=(b, h, nch),
        in_specs=in_specs,
        out_specs=[pl.BlockSpec((c, kv_), lambda bi, hi, ci: (row(bi, hi, ci), hi)),
                   pl.BlockSpec((None, None, kq, kv_), lambda bi, hi, ci: (bi, hi, 0, 0))],
        out_shape=[jax.ShapeDtypeStruct((b * t, h * kv_), BF16),
                   jax.ShapeDtypeStruct((b, h, kq, kv_), F32)],
        scratch_shapes=[pltpu.VMEM((kv_, kq), F32)],
        compiler_params=_cparams("parallel", "parallel", "arbitrary"),
    )(*args)


def _gla_layer(x, b, t, s0, norm_mix, w_in, b_r, w_a1, w_a2, b_a, norm_g, w_o):
    w_cat = jnp.concatenate([w_in, _pad_cols(w_a1)], axis=1).astype(BF16)
    p = _mm(x, w_cat, norm_g=norm_mix)
    w_a2p = jnp.pad(w_a2, ((0, LANES - GLA_GATE_RANK), (0, 0))).astype(BF16)
    o, s_new = _gla_scan(p, b, t, w_a2p, b_a, norm_g, b_r, s0)
    return _mm(o, w_o.astype(BF16), res=x), s_new


def _rel_bucket_np(dist):
    n = np.maximum(dist, 0)
    exact = N_BUCKETS // 2
    lg = np.log(np.maximum(n, 1).astype(np.float32) / np.float32(exact)) / np.float32(math.log(MAX_DISTANCE / exact))
    large = np.minimum(exact + (lg * np.float32(N_BUCKETS - exact)).astype(np.int32), N_BUCKETS - 1)
    return np.where(n < exact, n, large)


def rel_bucket(dist):
    n = jnp.maximum(dist, 0)
    exact = N_BUCKETS // 2
    lg = jnp.log(jnp.maximum(n, 1).astype(jnp.float32) / exact) / math.log(MAX_DISTANCE / exact)
    large = jnp.minimum(exact + (lg * (N_BUCKETS - exact)).astype(jnp.int32), N_BUCKETS - 1)
    return jnp.where(n < exact, n, large)


def _bias_lookup(rel_bias, d):
    onehot = jax.nn.one_hot(rel_bucket(d), N_BUCKETS, dtype=F32)
    return jnp.einsum('...b,bh->...h', onehot, rel_bias.astype(F32), precision=lax.Precision.HIGHEST)


def _toeplitz_bias(rel_bias, tq, tk, c):
    d = c + jnp.arange(tq, dtype=jnp.int32)[:, None] - jnp.arange(tk, dtype=jnp.int32)[None, :]
    return _bias_lookup(rel_bias, d).transpose(2, 0, 1).reshape(N_HEADS * tq, tk)


def _rope_tables(pos):
    half = MLA_ROPE // 2
    inv = ROPE_THETA ** (-jnp.arange(half, dtype=jnp.float32) / half)
    ang = pos.astype(jnp.float32)[:, None] * inv
    cos, sin = jnp.cos(ang), jnp.sin(ang)
    z = jnp.zeros_like(cos)
    zz = jnp.zeros((pos.shape[0], LANES - MLA_ROPE), F32)
    c = jnp.concatenate([cos, cos, zz], axis=1)
    sa = jnp.concatenate([-sin, z, zz], axis=1)
    sb = jnp.concatenate([z, sin, zz], axis=1)
    return c, sa, sb


def _rope128(x, c, sa, sb):
    return x * c + pltpu.roll(x, LANES - MLA_ROPE // 2, axis=1) * sa + pltpu.roll(x, MLA_ROPE // 2, axis=1) * sb


def _mla_latent_kernel(ckv_ref, kpe_ref, g_ref, c_ref, sa_ref, sb_ref, o_ref):
    x = ckv_ref[...]
    y = x * lax.rsqrt(jnp.mean(x * x, axis=-1, keepdims=True) + RMS_EPS) * g_ref[...]
    kp = _rope128(kpe_ref[...], c_ref[...], sa_ref[...], sb_ref[...])
    o_ref[:, 0:MLA_KV_RANK] = y
    o_ref[:, MLA_KV_RANK:MLA_KV_RANK + MLA_ROPE] = kp[:, 0:MLA_ROPE]


def _mla_q_kernel(xn_ref, xp_ref, w_ref, c_ref, sa_ref, sb_ref, qa_ref, qp_ref):
    qa_ref[...] = _dot(xn_ref[...].astype(BF16), w_ref[...]).astype(qa_ref.dtype)
    qp_ref[...] = _rope128(xp_ref[...], c_ref[...], sa_ref[...], sb_ref[...]).astype(qp_ref.dtype)


def _mla_project(x, t, pos, norm_mix, w_down, q_norm, w_uq, kv_norm, w_uk):
    m = x.shape[0]
    h = N_HEADS
    d = _mm(x, _pad_cols(w_down).astype(BF16), norm_g=norm_mix)
    w3 = w_uq.reshape(MLA_Q_RANK, h, MLA_NOPE + MLA_ROPE)
    w_n = w3[:, :, :MLA_NOPE].reshape(MLA_Q_RANK, h * MLA_NOPE)
    w_p = jnp.pad(w3[:, :, MLA_NOPE:], ((0, 0), (0, 0), (0, LANES - MLA_ROPE))).reshape(MLA_Q_RANK, h * LANES)
    q = _mm(d, jnp.concatenate([w_n, w_p], axis=1).astype(BF16), norm_g=q_norm)
    tabs = _rope_tables(pos)
    period = pos.shape[0]
    tm = _pick(m, 512, SUBLANES_BF16)
    if period < tm:
        assert tm % period == 0
        tabs = tuple(jnp.tile(tb, (tm // period, 1)) for tb in tabs)
        ntab = 1
    else:
        assert period % tm == 0
        ntab = period // tm
    latent = pl.pallas_call(
        _mla_latent_kernel,
        grid=(m // tm,),
        in_specs=[pl.BlockSpec((tm, MLA_KV_RANK), lambda i: (i, 1)),
                  pl.BlockSpec((tm, LANES), lambda i: (i, (MLA_Q_RANK + MLA_KV_RANK) // LANES)),
                  pl.BlockSpec((1, MLA_KV_RANK), lambda i: (0, 0))]
        + [pl.BlockSpec((tm, LANES), lambda i: (i % ntab, 0))] * 3,
        out_specs=pl.BlockSpec((tm, MLA_KV_RANK + MLA_ROPE), lambda i: (i, 0)),
        out_shape=jax.ShapeDtypeStruct((m, MLA_KV_RANK + MLA_ROPE), F32),
        compiler_params=_cparams("parallel"),
    )(d, d, kv_norm.reshape(1, MLA_KV_RANK), *tabs)
    w_ukt = w_uk.transpose(1, 2, 0).astype(BF16)
    qa, qp = pl.pallas_call(
        _mla_q_kernel,
        grid=(m // tm, h),
        in_specs=[pl.BlockSpec((tm, MLA_NOPE), lambda i, hi: (i, hi)),
                  pl.BlockSpec((tm, LANES), lambda i, hi: (i, h + hi)),
                  pl.BlockSpec((None, MLA_NOPE, MLA_KV_RANK), lambda i, hi: (hi, 0, 0))]
        + [pl.BlockSpec((tm, LANES), lambda i, hi: (i % ntab, 0))] * 3,
        out_specs=[pl.BlockSpec((None, tm, MLA_KV_RANK), lambda i, hi: (hi, i, 0)),
                   pl.BlockSpec((None, tm, LANES), lambda i, hi: (hi, i, 0))],
        out_shape=[jax.ShapeDtypeStruct((h, m, MLA_KV_RANK), BF16),
                   jax.ShapeDtypeStruct((h, m, LANES), BF16)],
        compiler_params=_cparams("parallel", "parallel"),
    )(q, q, w_ukt, *tabs)
    return qa, qp, latent


def _softmax_step(s, mask, m_ref, l_ref):
    z = jnp.where(mask, s, NEG_INF) if mask is not None else s
    m_old = m_ref[...]
    m_new = jnp.maximum(m_old, jnp.max(z, axis=-1, keepdims=True))
    p = jnp.exp(z - m_new)
    if mask is not None:
        p = jnp.where(mask, p, 0.0)
    alpha = jnp.exp(m_old - m_new)
    l_ref[...] = alpha * l_ref[...] + jnp.sum(p, axis=-1, keepdims=True)
    m_ref[...] = m_new
    return alpha, p


def _mla_finish(acc_ref, l_ref, wuv_ref, o_ref, tq):
    o_lat = (acc_ref[...] / jnp.maximum(l_ref[...], 1e-30)).astype(BF16)
    for hi in range(N_HEADS):
        o_ref[:, hi * MLA_V:(hi + 1) * MLA_V] = _dot(o_lat[hi * tq:(hi + 1) * tq], wuv_ref[hi]).astype(o_ref.dtype)


MLA_SCALE = (MLA_NOPE + MLA_ROPE) ** -0.5


def _mla_prompt_kernel(qa_ref, qp_ref, lat_ref, wuv_ref, o_ref, m_ref, l_ref, acc_ref, *, tq, tk):
    qt, kt = pl.program_id(1), pl.program_id(2)
    rows = N_HEADS * tq

    @pl.when(kt == 0)
    def _():
        m_ref[...] = jnp.full_like(m_ref, NEG_INF)
        l_ref[...] = jnp.zeros_like(l_ref)
        acc_ref[...] = jnp.zeros_like(acc_ref)

    @pl.when(kt * tk <= qt * tq + tq - 1)
    def _():
        kc = lat_ref[:, 0:MLA_KV_RANK].astype(BF16)
        kp = lat_ref[:, MLA_KV_RANK:MLA_KV_RANK + MLA_ROPE].astype(BF16)
        qa = qa_ref[...].reshape(rows, MLA_KV_RANK)
        qp = qp_ref[...].reshape(rows, LANES)[:, 0:MLA_ROPE]
        s = (_dot_nt(qa, kc) + _dot_nt(qp, kp)) * MLA_SCALE
        ti = qt * tq + lax.rem(lax.broadcasted_iota(jnp.int32, (rows, tk), 0), tq)
        kj = kt * tk + lax.broadcasted_iota(jnp.int32, (rows, tk), 1)
        alpha, p = _softmax_step(s, kj <= ti, m_ref, l_ref)
        acc_ref[...] = alpha * acc_ref[...] + _dot(p.astype(BF16), kc)

    @pl.when(kt == pl.num_programs(2) - 1)
    def _():
        _mla_finish(acc_ref, l_ref, wuv_ref, o_ref, tq)


def _mla_prompt_attend(qa, qp, latent, b, t, w_uv16):
    h = N_HEADS
    tq = _pick(t, 128, SUBLANES_BF16)
    tk = _pick(t, 512, 8)
    nq, nk = t // tq, t // tk
    rows = h * tq

    def kmap(bi, qt, kt):
        return (bi * nk + jnp.minimum(kt, (qt * tq + tq - 1) // tk), 0)

    return pl.pallas_call(
        functools.partial(_mla_prompt_kernel, tq=tq, tk=tk),
        grid=(b, nq, nk),
        in_specs=[pl.BlockSpec((h, tq, MLA_KV_RANK), lambda bi, qt, kt: (0, bi * nq + qt, 0)),
                  pl.BlockSpec((h, tq, LANES), lambda bi, qt, kt: (0, bi * nq + qt, 0)),
                  pl.BlockSpec((tk, MLA_KV_RANK + MLA_ROPE), kmap),
                  pl.BlockSpec((h, MLA_KV_RANK, MLA_V), lambda bi, qt, kt: (0, 0, 0))],
        out_specs=pl.BlockSpec((tq, h * MLA_V), lambda bi, qt, kt: (bi * nq + qt, 0)),
        out_shape=jax.ShapeDtypeStruct((b * t, h * MLA_V), BF16),
        scratch_shapes=[pltpu.VMEM((rows, 1), F32), pltpu.VMEM((rows, 1), F32), pltpu.VMEM((rows, MLA_KV_RANK), F32)],
        compiler_params=_cparams("parallel", "parallel", "arbitrary"),
    )(qa, qp, latent, w_uv16)


def _mla_sample_kernel(pt_ref, qa_ref, qp_ref, *rest, npp, ts):
    page_refs = rest[:npp]
    new_ref, wuv_ref, o_ref, m_ref, l_ref, acc_ref = rest[npp:]
    st = pl.program_id(1)
    rows = N_HEADS * ts

    @pl.when(st == 0)
    def _():
        m_ref[...] = jnp.full_like(m_ref, NEG_INF)
        l_ref[...] = jnp.zeros_like(l_ref)
        acc_ref[...] = jnp.zeros_like(acc_ref)

    qa = qa_ref[...]
    qp = qp_ref[:, 0:MLA_ROPE]
    kcs = [r[0:MLA_KV_RANK, :].astype(BF16) for r in page_refs]
    kps = [r[MLA_KV_RANK:MLA_KV_RANK + MLA_ROPE, :].astype(BF16) for r in page_refs]
    s = jnp.concatenate([_dot(qa, kc) + _dot(qp, kp) for kc, kp in zip(kcs, kps)], axis=1) * MLA_SCALE
    alpha, p = _softmax_step(s, None, m_ref, l_ref)
    p16 = p.astype(BF16)
    pv = _dot_nt(p16[:, 0:PAGE_SIZE], kcs[0])
    for j in range(1, npp):
        pv = pv + _dot_nt(p16[:, j * PAGE_SIZE:(j + 1) * PAGE_SIZE], kcs[j])
    acc_ref[...] = alpha * acc_ref[...] + pv

    @pl.when(st == pl.num_programs(1) - 1)
    def _():
        kc = new_ref[:, 0:MLA_KV_RANK].astype(BF16)
        kp = new_ref[:, MLA_KV_RANK:MLA_KV_RANK + MLA_ROPE].astype(BF16)
        s2 = (_dot_nt(qa, kc) + _dot_nt(qp, kp)) * MLA_SCALE
        ti = lax.rem(lax.broadcasted_iota(jnp.int32, (rows, ts), 0), ts)
        kj = lax.broadcasted_iota(jnp.int32, (rows, ts), 1)
        alpha2, p2 = _softmax_step(s2, kj <= ti, m_ref, l_ref)
        acc_ref[...] = alpha2 * acc_ref[...] + _dot(p2.astype(BF16), kc)
        _mla_finish(acc_ref, l_ref, wuv_ref, o_ref, ts)


def _mla_sample_attend(qa, qp, latent, cache, layer, page_table, b, ts, w_uv16):
    h = N_HEADS
    npg = page_table.shape[1]
    npp = _pick(npg, 16, 1)
    nst = npg // npp
    rows = h * ts
    qa_s = qa.reshape(h, b, ts, MLA_KV_RANK).transpose(1, 0, 2, 3).reshape(b, rows, MLA_KV_RANK)
    qp_s = qp.reshape(h, b, ts, LANES).transpose(1, 0, 2, 3).reshape(b, rows, LANES)
    lat3 = latent.reshape(b, ts, MLA_KV_RANK + MLA_ROPE)
    cache_t = cache.transpose(0, 1, 3, 2)

    def page_map(j):
        return lambda bi, st, pt: (layer, pt[bi * npg + st * npp + j], 0, 0)

    grid_spec = pltpu.PrefetchScalarGridSpec(
        num_scalar_prefetch=1,
        grid=(b, nst),
        in_specs=[pl.BlockSpec((None, rows, MLA_KV_RANK), lambda bi, st, pt: (bi, 0, 0)),
                  pl.BlockSpec((None, rows, LANES), lambda bi, st, pt: (bi, 0, 0))]
        + [pl.BlockSpec((None, None, MLA_KV_RANK + MLA_ROPE, PAGE_SIZE), page_map(j)) for j in range(npp)]
        + [pl.BlockSpec((None, ts, MLA_KV_RANK + MLA_ROPE), lambda bi, st, pt: (bi, 0, 0)),
           pl.BlockSpec((h, MLA_KV_RANK, MLA_V), lambda bi, st, pt: (0, 0, 0))],
        out_specs=pl.BlockSpec((None, ts, h * MLA_V), lambda bi, st, pt: (bi, 0, 0)),
        scratch_shapes=[pltpu.VMEM((rows, 1), F32), pltpu.VMEM((rows, 1), F32), pltpu.VMEM((rows, MLA_KV_RANK), F32)],
    )
    out = pl.pallas_call(
        functools.partial(_mla_sample_kernel, npp=npp, ts=ts),
        grid_spec=grid_spec,
        out_shape=jax.ShapeDtypeStruct((b, ts, h * MLA_V), BF16),
        compiler_params=_cparams("parallel", "arbitrary"),
    )(page_table.reshape(-1), qa_s, qp_s, *([cache_t] * npp), lat3, w_uv16)
    return out.reshape(b * ts, h * MLA_V)


def _mla_layer(x, b, t, pos, cache, layer, page_table, norm_mix, w_down, q_norm, w_uq, kv_norm, w_uk, w_uv, w_o):
    qa, qp, latent = _mla_project(x, t, pos, norm_mix, w_down, q_norm, w_uq, kv_norm, w_uk)
    w_uv16 = w_uv.transpose(1, 0, 2).astype(BF16)
    if cache is None:
        o = _mla_prompt_attend(qa, qp, latent, b, t, w_uv16)
    else:
        o = _mla_sample_attend(qa, qp, latent, cache, layer, page_table, b, t, w_uv16)
    return _mm(o, w_o.astype(BF16), res=x), latent.reshape(b, t, MLA_KV_RANK + MLA_ROPE)


MOBA_SCALE = HEAD_DIM ** -0.5
MOBA_HPG = N_HEADS // MOBA_KV_HEADS


def _group_rows(q_ref, g, hpg):
    return jnp.concatenate([q_ref[:, (g * hpg + j) * HEAD_DIM:(g * hpg + j + 1) * HEAD_DIM] for j in range(hpg)],
                           axis=0)


def _moba_kernel(pt_ref, q_ref, *rest, tq, pos0, n_past, paged, own_keys, bps):
    ppb = MOBA_BLOCK // PAGE_SIZE
    nkv = ppb * bps if paged else 2
    kv_refs = rest[:nkv]
    own_refs = rest[nkv:nkv + 2]
    bp_ref, bo_ref, o_ref, m_all, l_all, g_all, o_all = rest[nkv + 2:]
    qt, n = pl.program_id(1), pl.program_id(2)
    hpg, grp = MOBA_HPG, MOBA_KV_HEADS
    gw = grp * HEAD_DIM
    rg = hpg * tq
    t0 = pos0 + qt * tq
    own = t0 // MOBA_BLOCK

    @pl.when(n == 0)
    def _():
        m_all[...] = jnp.full_like(m_all, NEG_INF)
        g_all[...] = jnp.full_like(g_all, NEG_INF)
        l_all[...] = jnp.zeros_like(l_all)
        o_all[...] = jnp.zeros_like(o_all)

    @pl.when(jnp.logical_and(n < n_past // bps, n * bps < own))
    def _():
        lane = lax.broadcasted_iota(jnp.int32, (rg, LANES), 1)
        for g in range(grp):
            rows = slice(g * rg, (g + 1) * rg)
            qg = _group_rows(q_ref, g, hpg)
            q2 = jnp.concatenate([qg, qg * MOBA_SCALE], axis=0).astype(BF16)
            if paged:
                kg = jnp.concatenate([r[pl.ds(g, PAGE_SIZE, stride=2 * grp), :] for r in kv_refs], axis=0)
                vg = jnp.concatenate([r[pl.ds(grp + g, PAGE_SIZE, stride=2 * grp), :] for r in kv_refs], axis=0)
            else:
                kg = kv_refs[0][:, g * HEAD_DIM:(g + 1) * HEAD_DIM]
                vg = kv_refs[1][:, g * HEAD_DIM:(g + 1) * HEAD_DIM]
            kg, vg = kg.astype(BF16), vg.astype(BF16)
            both = _dot_nt(q2, kg)
            m_new, l_new, g_new = m_all[rows, :], l_all[rows, :], g_all[rows, :]
            for i in range(bps):
                blk = n * bps + i
                cols = slice(i * MOBA_BLOCK, (i + 1) * MOBA_BLOCK)
                bidx = jnp.where(blk == n_past - 1, 1, 0) if paged else 0
                gate = jnp.sum(both[0:rg, cols], axis=-1, keepdims=True) * (1.0 / MOBA_BLOCK)
                s = both[rg:2 * rg, cols] + bp_ref[bidx, rows, :]
                mx = jnp.max(s, axis=-1, keepdims=True)
                p = jnp.exp(s - mx)
                hit = lane == blk
                m_new = jnp.where(hit, mx, m_new)
                l_new = jnp.where(hit, jnp.sum(p, axis=-1, keepdims=True), l_new)
                g_new = jnp.where(hit, gate, g_new)
                o_all[blk, rows, :] = _dot(p.astype(BF16), vg[i * MOBA_BLOCK:(i + 1) * MOBA_BLOCK])
            m_all[rows, :] = m_new
            l_all[rows, :] = l_new
            g_all[rows, :] = g_new

    @pl.when(n == pl.num_programs(2) - 1)
    def _():
        ko, vo = own_refs[0][...], own_refs[1][...]
        nko = bo_ref.shape[1]
        if own_keys < nko:
            zpad = jnp.zeros((nko - own_keys, gw), F32)
            ko = jnp.concatenate([ko, zpad], axis=0)
            vo = jnp.concatenate([vo, zpad], axis=0)
        ti = t0 + lax.rem(lax.broadcasted_iota(jnp.int32, (rg, nko), 0), tq)
        kj = own * MOBA_BLOCK + lax.broadcasted_iota(jnp.int32, (rg, nko), 1)
        mask = kj <= ti
        lane = lax.broadcasted_iota(jnp.int32, (rg, LANES), 1)
        for g in range(grp):
            rows = slice(g * rg, (g + 1) * rg)
            qg = _group_rows(q_ref, g, hpg)
            kg = ko[:, g * HEAD_DIM:(g + 1) * HEAD_DIM].astype(BF16)
            vg = vo[:, g * HEAD_DIM:(g + 1) * HEAD_DIM].astype(BF16)
            s = _dot_nt((qg * MOBA_SCALE).astype(BF16), kg) + bo_ref[rows, :]
            z = jnp.where(mask, s, NEG_INF)
            m_o = jnp.max(z, axis=-1, keepdims=True)
            p = jnp.where(mask, jnp.exp(z - m_o), 0.0)
            l_o = jnp.sum(p, axis=-1, keepdims=True)
            o_o = _dot(p.astype(BF16), vg)
            gg = g_all[rows, :]
            sel = jnp.zeros((rg, LANES), jnp.bool_)
            for _ in range(MOBA_TOPK):
                mx = jnp.max(gg, axis=-1, keepdims=True)
                idx = jnp.min(jnp.where(gg == mx, lane, LANES), axis=-1, keepdims=True)
                hit = lane == idx
                sel = jnp.logical_or(sel, jnp.logical_and(hit, mx > NEG_INF / 2))
                gg = jnp.where(hit, -3e38, gg)
            mm = m_all[rows, :]
            m_tot = jnp.maximum(m_o, jnp.max(jnp.where(sel, mm, NEG_INF), axis=-1, keepdims=True))
            w = jnp.where(sel, jnp.exp(mm - m_tot), 0.0)
            w_o = jnp.exp(m_o - m_tot)
            l_tot = jnp.sum(w * l_all[rows, :], axis=-1, keepdims=True) + w_o * l_o
            o = w_o * o_o
            for nb in range(n_past):
                o = o + w[:, nb:nb + 1] * o_all[nb, rows, :]
            o = o / jnp.maximum(l_tot, 1e-30)
            for j in range(hpg):
                hh = g * hpg + j
                o_ref[:, hh * HEAD_DIM:(hh + 1) * HEAD_DIM] = o[j * tq:(j + 1) * tq].astype(o_ref.dtype)


def _moba_attend(p, b, t, pos0, rel_bias, cache, layer, page_table):
    h, grp = N_HEADS, MOBA_KV_HEADS
    gw = grp * HEAD_DIM
    paged = cache is not None
    assert pos0 % MOBA_BLOCK == 0
    if paged:
        assert t <= MOBA_BLOCK
        tq, nq = t, 1
        n_past = pos0 // MOBA_BLOCK
        own_keys = t
        nko = LANES
        ppb = MOBA_BLOCK // PAGE_SIZE
        npg = page_table.shape[1]
        pt = page_table.reshape(-1)
        cache4 = cache.reshape(cache.shape[0], cache.shape[1], PAGE_SIZE * 2 * grp, HEAD_DIM)

        bps = _pick(n_past, 4, 1)
        n_steps = n_past // bps

        def page_map(j):
            return lambda bi, qt, n, pt_: (layer, pt_[bi * npg + jnp.minimum(n, n_steps - 1) * bps * ppb + j], 0, 0)

        kv_specs = [pl.BlockSpec((None, None, PAGE_SIZE * 2 * grp, HEAD_DIM), page_map(j)) for j in range(bps * ppb)]
        kv_args = [cache4] * (bps * ppb)
        own_specs = [pl.BlockSpec((t, gw), lambda bi, qt, n, pt_: (bi, h * HEAD_DIM // gw)),
                     pl.BlockSpec((t, gw), lambda bi, qt, n, pt_: (bi, h * HEAD_DIM // gw + 1))]
        bias_past = jnp.stack([jnp.broadcast_to(jnp.repeat(rel_bias[N_BUCKETS - 1], tq)[:, None], (h * tq, MOBA_BLOCK)),
                               _toeplitz_bias(rel_bias, tq, MOBA_BLOCK, MOBA_BLOCK)])
        bias_own = _toeplitz_bias(rel_bias, tq, nko, 0)[None]
        bp_spec = pl.BlockSpec((2, h * tq, MOBA_BLOCK), lambda bi, qt, n, pt_: (0, 0, 0))

        def bo_map(bi, qt, n, pt_):
            return (0, 0, 0)
    else:
        assert t % MOBA_BLOCK == 0
        tq = 128
        nq = t // tq
        nblk = t // MOBA_BLOCK
        n_past = nblk - 1
        own_keys = MOBA_BLOCK
        nko = MOBA_BLOCK
        pt = jnp.zeros((1,), jnp.int32)
        qpb = MOBA_BLOCK // tq

        def past_map(col):
            return lambda bi, qt, n, pt_: (bi * nblk + jnp.minimum(n, jnp.maximum(qt // qpb - 1, 0)), col)

        kcol = h * HEAD_DIM // gw
        kv_specs = [pl.BlockSpec((MOBA_BLOCK, gw), past_map(kcol)), pl.BlockSpec((MOBA_BLOCK, gw), past_map(kcol + 1))]
        kv_args = [p, p]
        own_specs = [pl.BlockSpec((MOBA_BLOCK, gw), lambda bi, qt, n, pt_: (bi * nblk + qt // qpb, kcol)),
                     pl.BlockSpec((MOBA_BLOCK, gw), lambda bi, qt, n, pt_: (bi * nblk + qt // qpb, kcol + 1))]
        far = jnp.broadcast_to(jnp.repeat(rel_bias[N_BUCKETS - 1], tq)[:, None], (h * tq, MOBA_BLOCK))
        bias_past = jnp.stack([far] + [_toeplitz_bias(rel_bias, tq, MOBA_BLOCK, MOBA_BLOCK + r * tq) for r in range(qpb)])
        bias_own = jnp.stack([_toeplitz_bias(rel_bias, tq, MOBA_BLOCK, r * tq) for r in range(qpb)])

        bps, n_steps = 1, n_past
        bp_spec = pl.BlockSpec((1, h * tq, MOBA_BLOCK),
                               lambda bi, qt, n, pt_: (jnp.where(n == qt // qpb - 1, 1 + qt % qpb, 0), 0, 0))

        def bo_map(bi, qt, n, pt_):
            return (qt % qpb, 0, 0)

    rows = h * tq
    grid_spec = pltpu.PrefetchScalarGridSpec(
        num_scalar_prefetch=1,
        grid=(b, nq, n_steps + 1),
        in_specs=[pl.BlockSpec((tq, h * HEAD_DIM), lambda bi, qt, n, pt_: (bi * nq + qt, 0))]
        + kv_specs + own_specs
        + [bp_spec, pl.BlockSpec((None, rows, nko), bo_map)],
        out_specs=pl.BlockSpec((tq, h * HEAD_DIM), lambda bi, qt, n, pt_: (bi * nq + qt, 0)),
        scratch_shapes=[pltpu.VMEM((rows, LANES), F32), pltpu.VMEM((rows, LANES), F32), pltpu.VMEM((rows, LANES), F32),
                        pltpu.VMEM((max(n_past, 1), rows, HEAD_DIM), F32)],
    )
    assert n_past <= LANES
    return pl.pallas_call(
        functools.partial(_moba_kernel, tq=tq, pos0=pos0, n_past=n_past, paged=paged, own_keys=own_keys, bps=bps),
        grid_spec=grid_spec,
        out_shape=jax.ShapeDtypeStruct((b * t, h * HEAD_DIM), BF16),
        compiler_params=_cparams("parallel", "parallel", "arbitrary"),
    )(pt, p, *kv_args, p, p, bias_past, bias_own)


def _moba_layer(x, b, t, pos0, rel_bias, cache, layer, page_table, norm_mix, w_in, w_o):
    p = _mm(x, w_in.astype(BF16), norm_g=norm_mix)
    o = _moba_attend(p, b, t, pos0, rel_bias, cache, layer, page_table)
    kv = p[:, N_HEADS * HEAD_DIM:].reshape(b, t, 2, MOBA_KV_HEADS, HEAD_DIM)
    return _mm(o, w_o.astype(BF16), res=x), kv


NSA_HPG = N_HEADS // NSA_KV_HEADS
NSA_SCALE = HEAD_DIM ** -0.5
NSA_KV_COL0 = N_HEADS * HEAD_DIM // LANES
NSA_CACHE_ROWS = 4 * NSA_KV_HEADS
NSA_WIN_ROWS = 2 * NSA_KV_HEADS
NSA_PARTS = NSA_CMP_BLOCK // NSA_CMP_STRIDE
NSA_SEL_SHIFT = NSA_SEL_BLOCK.bit_length() - 1
assert 1 << NSA_SEL_SHIFT == NSA_SEL_BLOCK and NSA_PARTS == 2


def _dot3(a, b16):
    a1 = a.astype(BF16)
    r1 = a - a1.astype(F32)
    a2 = r1.astype(BF16)
    a3 = (r1 - a2.astype(F32)).astype(BF16)
    return _dot(a1, b16) + _dot(a2, b16) + _dot(a3, b16)


def _far_bias(rel_bias, tq, kb):
    return jnp.broadcast_to(jnp.repeat(rel_bias[N_BUCKETS - 1].astype(F32), tq)[:, None], (N_HEADS * tq, kb))


def _nsa_cmp_kernel(pt_ref, *refs, n_src, paged, cps):
    srcs = refs[:n_src]
    w_ref, o_ref = refs[n_src:]
    for c in range(2 * NSA_KV_HEADS):
        acc = jnp.zeros((cps, 2 * HEAD_DIM), F32)
        for s in range(NSA_CMP_STRIDE):
            if paged:
                xs = jnp.concatenate(
                    [r[pl.ds(s * NSA_CACHE_ROWS + c, PAGE_SIZE // NSA_CMP_STRIDE, stride=NSA_CMP_STRIDE * NSA_CACHE_ROWS), :]
                     for r in srcs], axis=0)
            else:
                xs = srcs[c][pl.ds(s, cps, stride=NSA_CMP_STRIDE), :]
            acc = acc + _dot(xs.astype(BF16), w_ref[c // NSA_KV_HEADS, s])
        o_ref[c] = acc


def _nsa_compress(p, b, t, pos0, pool, layer, page_table, w_cmp):
    ncomb = 2 * NSA_KV_HEADS
    wc = w_cmp.reshape(2, NSA_PARTS, NSA_CMP_STRIDE, HEAD_DIM, HEAD_DIM).transpose(0, 2, 3, 1, 4)
    wc = wc.reshape(2, NSA_CMP_STRIDE, HEAD_DIM, NSA_PARTS * HEAD_DIM).astype(BF16)
    w_spec = pl.BlockSpec(wc.shape, lambda bi, st, pt_: (0, 0, 0, 0))
    if pool is None:
        assert t % NSA_CMP_STRIDE == 0
        n_chunk = t // NSA_CMP_STRIDE
        cps, nst = n_chunk, 1
        pt = jnp.zeros((1,), jnp.int32)
        srcs = [p] * ncomb
        src_specs = [pl.BlockSpec((t, HEAD_DIM), (lambda c: lambda bi, st, pt_: (bi, NSA_KV_COL0 + c))(c))
                     for c in range(ncomb)]
    else:
        assert pos0 % NSA_CMP_STRIDE == 0 and t < NSA_CMP_STRIDE
        npg = page_table.shape[1]
        npp = _pick(npg, 16, 1)
        nst = npg // npp
        cps = npp * (PAGE_SIZE // NSA_CMP_STRIDE)
        n_chunk = nst * cps
        pt = page_table.reshape(-1)
        pool4 = pool.reshape(pool.shape[0], pool.shape[1], PAGE_SIZE * NSA_CACHE_ROWS, HEAD_DIM)
        srcs = [pool4] * npp
        src_specs = [pl.BlockSpec((None, None, PAGE_SIZE * NSA_CACHE_ROWS, HEAD_DIM),
                                  (lambda j: lambda bi, st, pt_: (layer, pt_[bi * npg + st * npp + j], 0, 0))(j))
                     for j in range(npp)]
    grid_spec = pltpu.PrefetchScalarGridSpec(
        num_scalar_prefetch=1,
        grid=(b, nst),
        in_specs=src_specs + [w_spec],
        out_specs=pl.BlockSpec((None, ncomb, cps, NSA_PARTS * HEAD_DIM), lambda bi, st, pt_: (bi, 0, st, 0)),
    )
    return pl.pallas_call(
        functools.partial(_nsa_cmp_kernel, n_src=len(srcs), paged=pool is not None, cps=cps),
        grid_spec=grid_spec,
        out_shape=jax.ShapeDtypeStruct((b, ncomb, n_chunk, NSA_PARTS * HEAD_DIM), F32),
        compiler_params=_cparams("parallel", "parallel"),
    )(pt, *srcs, wc)


def _topk_mask(score, lane, k):
    sel = jnp.zeros(score.shape, jnp.bool_)
    width = score.shape[-1]
    for _ in range(k):
        mx = jnp.max(score, axis=-1, keepdims=True)
        idx = jnp.min(jnp.where(score == mx, lane, width), axis=-1, keepdims=True)
        hit = lane == idx
        sel = jnp.logical_or(sel, jnp.logical_and(hit, mx > NEG_INF / 2))
        score = jnp.where(hit, -3e38, score)
    return sel


def _nsa_select_kernel(q_ref, pr_ref, bc_ref, bias_ref, imap_ref, oc_ref, sel_ref, *, tq, pos0, n_cmp, n_top):
    qt = pl.program_id(1)
    grp, hpg = NSA_KV_HEADS, NSA_HPG
    ncp, nsp = pr_ref.shape[1], imap_ref.shape[1]
    rg = hpg * tq
    t0 = pos0 + qt * tq
    t_row = t0 + lax.rem(lax.broadcasted_iota(jnp.int32, (rg, ncp), 0), tq)
    cidx = lax.broadcasted_iota(jnp.int32, (rg, ncp), 1)
    mask = jnp.logical_and(cidx * NSA_CMP_STRIDE + (NSA_CMP_BLOCK - 1) <= t_row, cidx < n_cmp)
    ti = t0 + lax.broadcasted_iota(jnp.int32, (tq, nsp), 0)
    blk = lax.broadcasted_iota(jnp.int32, (tq, nsp), 1)
    tb = jnp.right_shift(ti, NSA_SEL_SHIFT)
    forced = jnp.logical_or(blk == 0, jnp.logical_or(blk == tb, blk == tb - 1))
    for g in range(grp):
        pk, pv = pr_ref[g], pr_ref[grp + g]
        kc = bc_ref[0:1] + pk[:, 0:HEAD_DIM] + pltpu.roll(pk[:, HEAD_DIM:2 * HEAD_DIM], ncp - 1, axis=0)
        vc = bc_ref[1:2] + pv[:, 0:HEAD_DIM] + pltpu.roll(pv[:, HEAD_DIM:2 * HEAD_DIM], ncp - 1, axis=0)
        qg = (_group_rows(q_ref, g, hpg) * NSA_SCALE).astype(BF16)
        lg = _dot_nt(qg, kc.astype(BF16)) + bias_ref[g * rg:(g + 1) * rg, :]
        z = jnp.where(mask, lg, NEG_INF)
        z = z - jnp.max(z, axis=-1, keepdims=True)
        e = jnp.where(mask, jnp.exp(z), 0.0)
        pc = e / jnp.maximum(jnp.sum(e, axis=-1, keepdims=True), 1e-30)
        oc = _dot(pc.astype(BF16), vc.astype(BF16))
        psum = pc[0:tq]
        for j in range(hpg):
            hh = g * hpg + j
            oc_ref[:, hh * HEAD_DIM:(hh + 1) * HEAD_DIM] = oc[j * tq:(j + 1) * tq]
            if j:
                psum = psum + pc[j * tq:(j + 1) * tq]
        imp = _dot3(psum, imap_ref[...])
        score = jnp.where(blk <= tb, jnp.where(forced, NSA_FORCE, imp), NEG_INF)
        sel_ref[:, g * nsp:(g + 1) * nsp] = _topk_mask(score, blk, n_top).astype(F32)


def _nsa_select(p, proj, b, t, pos0, rel_bias, b_cmp, tq):
    h, grp = N_HEADS, NSA_KV_HEADS
    nq = t // tq
    ncp = proj.shape[2]
    seq_len = pos0 + t
    n_cmp = seq_len // NSA_CMP_STRIDE - NSA_PARTS + 1
    n_sel = -(-seq_len // NSA_SEL_BLOCK)
    nsp = -(-n_sel // LANES) * LANES
    assert n_cmp <= ncp
    c0 = np.arange(ncp)[:, None] * NSA_CMP_STRIDE
    s0 = np.arange(nsp)[None, :] * NSA_SEL_BLOCK
    inter = np.minimum(c0 + NSA_CMP_BLOCK, s0 + NSA_SEL_BLOCK) - np.maximum(c0, s0)
    imap = np.maximum(inter, 0) / NSA_CMP_BLOCK
    imap = np.where((np.arange(ncp)[:, None] < n_cmp) & (np.arange(nsp)[None, :] < n_sel), imap, 0.0)
    imap = jnp.asarray(imap, F32).astype(BF16)
    tpos = pos0 + jnp.arange(nq * tq, dtype=jnp.int32).reshape(nq, tq)
    d = tpos[:, :, None] - (jnp.arange(ncp, dtype=jnp.int32) * NSA_CMP_STRIDE + (NSA_CMP_BLOCK - 1))[None, None, :]
    bias = _bias_lookup(rel_bias, d).transpose(0, 3, 1, 2).reshape(nq, h * tq, ncp)
    return pl.pallas_call(
        functools.partial(_nsa_select_kernel, tq=tq, pos0=pos0, n_cmp=n_cmp, n_top=min(NSA_N_SEL, n_sel)),
        grid=(b, nq),
        in_specs=[pl.BlockSpec((tq, h * HEAD_DIM), lambda bi, qt: (bi * nq + qt, 0)),
                  pl.BlockSpec((None, 2 * grp, ncp, NSA_PARTS * HEAD_DIM), lambda bi, qt: (bi, 0, 0, 0)),
                  pl.BlockSpec((2, HEAD_DIM), lambda bi, qt: (0, 0)),
                  pl.BlockSpec((None, h * tq, ncp), lambda bi, qt: (qt, 0, 0)),
                  pl.BlockSpec((ncp, nsp), lambda bi, qt: (0, 0))],
        out_specs=[pl.BlockSpec((tq, h * HEAD_DIM), lambda bi, qt: (bi * nq + qt, 0)),
                   pl.BlockSpec((tq, grp * nsp), lambda bi, qt: (bi * nq + qt, 0))],
        out_shape=[jax.ShapeDtypeStruct((b * t, h * HEAD_DIM), F32),
                   jax.ShapeDtypeStruct((b * t, grp * nsp), F32)],
        compiler_params=_cparams("parallel", "parallel"),
    )(p, proj, b_cmp, bias, imap)


def _softmax_rows(s, mask, m_ref, l_ref, rows):
    z = jnp.where(mask, s, NEG_INF)
    m_old = m_ref[rows, :]
    m_new = jnp.maximum(m_old, jnp.max(z, axis=-1, keepdims=True))
    p = jnp.where(mask, jnp.exp(z - m_new), 0.0)
    alpha = jnp.exp(m_old - m_new)
    l_ref[rows, :] = alpha * l_ref[rows, :] + jnp.sum(p, axis=-1, keepdims=True)
    m_ref[rows, :] = m_new
    return alpha, p


def _fa_tile(qt, kt, *, tq, kb, n_main, pos0, k_base, rel_window):
    t0 = pos0 + qt * tq
    tile = (t0 - k_base) // kb - (n_main - 1) + kt if rel_window else kt
    k0 = k_base + tile * kb
    active = jnp.logical_and(tile >= 0, k0 <= t0 + tq - 1)
    return t0, tile, k0, active


def _nsa_fa_kernel(pt_ref, *refs, kind, kv_mode, tq, kb, n_main, pos0, k_base, rel_window, tail, ts, win_lo, nsp, tail_blk):
    grp, hpg = NSA_KV_HEADS, NSA_HPG
    it = iter(refs)
    q_ref = next(it)
    if kind == 'sel':
        sel_ref, e_ref = next(it), next(it)
    n_kv = {'cols': 2 * grp, 'paged': kb // PAGE_SIZE, 'buf': 1}[kv_mode]
    kv = [next(it) for _ in range(n_kv)]
    tl = [next(it) for _ in range(2 * grp)] if tail else None
    bias_ref = next(it)
    bt_ref = next(it) if tail else None
    o_ref, m_ref, l_ref, acc_ref = next(it), next(it), next(it), next(it)
    qt, kt = pl.program_id(1), pl.program_id(2)
    t0, _, k0, active = _fa_tile(qt, kt, tq=tq, kb=kb, n_main=n_main, pos0=pos0, k_base=k_base, rel_window=rel_window)
    rg = hpg * tq

    @pl.when(kt == 0)
    def _():
        m_ref[...] = jnp.full_like(m_ref, NEG_INF)
        l_ref[...] = jnp.zeros_like(l_ref)
        acc_ref[...] = jnp.zeros_like(acc_ref)

    def load(g, which):
        if kv_mode == 'cols':
            return kv[which * grp + g][...]
        if kv_mode == 'paged':
            off = (2 + which) * grp + g
            return jnp.concatenate([r[pl.ds(off, PAGE_SIZE, stride=NSA_CACHE_ROWS), :] for r in kv], axis=0)
        return kv[0][pl.ds(which * grp + g, kb, stride=NSA_WIN_ROWS), :]

    def base_mask(first_key, nk):
        ti = t0 + lax.rem(lax.broadcasted_iota(jnp.int32, (rg, nk), 0), tq)
        kj = first_key + lax.broadcasted_iota(jnp.int32, (rg, nk), 1)
        m = kj <= ti
        if kind == 'win':
            m = jnp.logical_and(m, jnp.logical_and(ti - kj <= NSA_WINDOW, kj >= win_lo))
        return m

    def attend(g, kg, vg, bias, m):
        rows = slice(g * rg, (g + 1) * rg)
        qg = (_group_rows(q_ref, g, hpg) * NSA_SCALE).astype(BF16)
        s = _dot_nt(qg, kg.astype(BF16)) + bias
        alpha, p = _softmax_rows(s, m, m_ref, l_ref, rows)
        acc_ref[rows, :] = alpha * acc_ref[rows, :] + _dot(p.astype(BF16), vg.astype(BF16))

    @pl.when(active)
    def _():
        base = base_mask(k0, kb)
        for g in range(grp):
            m = base
            if kind == 'sel':
                ex = _dot(sel_ref[:, g * nsp:(g + 1) * nsp].astype(BF16), e_ref[...])
                m = jnp.logical_and(m, jnp.concatenate([ex] * hpg, axis=0) > 0.5)
            attend(g, load(g, 0), load(g, 1), bias_ref[g * rg:(g + 1) * rg, :], m)

    @pl.when(kt == n_main - 1)
    def _():
        if tail:
            base = base_mask(pos0, LANES)
            zpad = jnp.zeros((LANES - ts, HEAD_DIM), F32)
            for g in range(grp):
                m = base
                if kind == 'sel':
                    col = sel_ref[:, g * nsp + tail_blk:g * nsp + tail_blk + 1]
                    m = jnp.logical_and(m, jnp.concatenate([col] * hpg, axis=0) > 0.5)
                kg = jnp.concatenate([tl[g][...], zpad], axis=0)
                vg = jnp.concatenate([tl[grp + g][...], zpad], axis=0)
                attend(g, kg, vg, bt_ref[g * rg:(g + 1) * rg, :], m)
        o = acc_ref[...] / jnp.maximum(l_ref[...], 1e-30)
        for hh in range(N_HEADS):
            o_ref[:, hh * HEAD_DIM:(hh + 1) * HEAD_DIM] = o[hh * tq:(hh + 1) * tq]


def _nsa_fa(kind, p, sel, b, t, pos0, rel_bias, pool, layer, page_table, win_buf, tq):
    h, grp = N_HEADS, NSA_KV_HEADS
    nq = t // tq
    kinds = (2, 3) if kind == 'sel' else (4, 5)
    sample = page_table is not None
    seq_len = pos0 + t
    nsp = -(-(-(-seq_len // NSA_SEL_BLOCK)) // LANES) * LANES
    if not sample:
        kb = 2 * tq
        assert t % kb == 0 and NSA_WINDOW % kb == 0
        n_main = t // kb if kind == 'sel' else NSA_WINDOW // kb + 1
        cfg = dict(tq=tq, kb=kb, n_main=n_main, pos0=0, k_base=0, rel_window=kind == 'win')
        cs = (0, tq, 2 * tq, 3 * tq)
        kv_mode, tail, tail_blk, win_lo = 'cols', False, 0, 0
        ntile = t // kb
        pt = jnp.zeros((1,), jnp.int32)
    else:
        assert t <= NSA_SEL_BLOCK and pos0 % NSA_SEL_BLOCK == 0
        tail, tail_blk = True, pos0 // NSA_SEL_BLOCK
        pt = page_table.reshape(-1)
        npg = page_table.shape[1]
        if kind == 'sel':
            kb = _pick(pos0, 1024, PAGE_SIZE)
            cfg = dict(tq=tq, kb=kb, n_main=pos0 // kb, pos0=pos0, k_base=0, rel_window=False)
            kv_mode, win_lo = 'paged', 0
        else:
            kb = win_buf.shape[1]
            cfg = dict(tq=tq, kb=kb, n_main=1, pos0=pos0, k_base=pos0 - kb, rel_window=False)
            kv_mode, win_lo = 'buf', pos0 - kb
        cs = (kb,)
        ntile = cfg['n_main']
    n_main = cfg['n_main']

    def tile_c(qt, kt):
        t0, tile, _, _ = _fa_tile(qt, kt, **cfg)
        return jnp.clip(tile, 0, jnp.minimum((t0 + tq - 1 - cfg['k_base']) // kb, ntile - 1))

    def bias_idx(bi, qt, kt, pt_):
        t0, _, k0, _ = _fa_tile(qt, kt, **cfg)
        idx = 0
        for i, c in enumerate(cs):
            idx = idx + jnp.where(t0 - k0 == c, i + 1, 0)
        return (idx, 0, 0)

    in_specs = [pl.BlockSpec((tq, h * HEAD_DIM), lambda bi, qt, kt, pt_: (bi * nq + qt, 0))]
    args = [p]
    if kind == 'sel':
        e = (np.arange(nsp)[None, :, None] == ((np.arange(ntile)[:, None, None] * kb + np.arange(kb)[None, None, :])
                                                 // NSA_SEL_BLOCK))
        in_specs += [pl.BlockSpec((tq, grp * nsp), lambda bi, qt, kt, pt_: (bi * nq + qt, 0)),
                     pl.BlockSpec((None, nsp, kb), lambda bi, qt, kt, pt_: (tile_c(qt, kt), 0, 0))]
        args += [sel, jnp.asarray(e, F32).astype(BF16)]
    if kv_mode == 'cols':
        for kd in kinds:
            for g in range(grp):
                col = NSA_KV_COL0 + kd * grp + g
                in_specs.append(pl.BlockSpec((kb, HEAD_DIM), (lambda col: lambda bi, qt, kt, pt_: (bi * ntile + tile_c(qt, kt), col))(col)))
                args.append(p)
    elif kv_mode == 'paged':
        ppt = kb // PAGE_SIZE
        pool4 = pool.reshape(pool.shape[0], pool.shape[1], PAGE_SIZE * NSA_CACHE_ROWS, HEAD_DIM)
        for j in range(ppt):
            in_specs.append(pl.BlockSpec((None, None, PAGE_SIZE * NSA_CACHE_ROWS, HEAD_DIM),
                                         (lambda j: lambda bi, qt, kt, pt_: (layer, pt_[bi * npg + kt * ppt + j], 0, 0))(j)))
            args.append(pool4)
    else:
        in_specs.append(pl.BlockSpec((None, kb * NSA_WIN_ROWS, HEAD_DIM), lambda bi, qt, kt, pt_: (bi, 0, 0)))
        args.append(win_buf.reshape(b, kb * NSA_WIN_ROWS, HEAD_DIM))
    if tail:
        for kd in kinds:
            for g in range(grp):
                col = NSA_KV_COL0 + kd * grp + g
                in_specs.append(pl.BlockSpec((t, HEAD_DIM), (lambda col: lambda bi, qt, kt, pt_: (bi, col))(col)))
                args.append(p)
    in_specs.append(pl.BlockSpec((None, h * tq, kb), bias_idx))
    args.append(jnp.stack([_far_bias(rel_bias, tq, kb)] + [_toeplitz_bias(rel_bias, tq, kb, c) for c in cs]))
    if tail:
        in_specs.append(pl.BlockSpec((h * tq, LANES), lambda bi, qt, kt, pt_: (0, 0)))
        args.append(_toeplitz_bias(rel_bias, tq, LANES, 0))
    grid_spec = pltpu.PrefetchScalarGridSpec(
        num_scalar_prefetch=1,
        grid=(b, nq, n_main),
        in_specs=in_specs,
        out_specs=pl.BlockSpec((tq, h * HEAD_DIM), lambda bi, qt, kt, pt_: (bi * nq + qt, 0)),
        scratch_shapes=[pltpu.VMEM((h * tq, 1), F32), pltpu.VMEM((h * tq, 1), F32), pltpu.VMEM((h * tq, HEAD_DIM), F32)],
    )
    return pl.pallas_call(
        functools.partial(_nsa_fa_kernel, kind=kind, kv_mode=kv_mode, tail=tail, ts=t, win_lo=win_lo, nsp=nsp,
                          tail_blk=tail_blk, **cfg),
        grid_spec=grid_spec,
        out_shape=jax.ShapeDtypeStruct((b * t, h * HEAD_DIM), F32),
        compiler_params=_cparams("parallel", "parallel", "arbitrary"),
    )(pt, *args)


def _nsa_merge_kernel(gl_ref, bg_ref, oc_ref, os_ref, ow_ref, o_ref):
    gates = _sigmoid(gl_ref[...] + bg_ref[...])
    for hh in range(N_HEADS):
        cols = slice(hh * HEAD_DIM, (hh + 1) * HEAD_DIM)
        o = (gates[:, 3 * hh:3 * hh + 1] * oc_ref[:, cols] + gates[:, 3 * hh + 1:3 * hh + 2] * os_ref[:, cols]
             + gates[:, 3 * hh + 2:3 * hh + 3] * ow_ref[:, cols])
        o_ref[:, cols] = o.astype(o_ref.dtype)


def _nsa_merge(p, b_gate, o_c, o_s, o_w):
    m = p.shape[0]
    d = N_HEADS * HEAD_DIM
    tm = _pick(m, 256, SUBLANES_BF16)
    gcol = (d + 6 * NSA_KV_HEADS * HEAD_DIM) // LANES
    o_spec = pl.BlockSpec((tm, d), lambda i: (i, 0))
    return pl.pallas_call(
        _nsa_merge_kernel,
        grid=(m // tm,),
        in_specs=[pl.BlockSpec((tm, LANES), lambda i: (i, gcol)), pl.BlockSpec((1, LANES), lambda i: (0, 0)),
                  o_spec, o_spec, o_spec],
        out_specs=o_spec,
        out_shape=jax.ShapeDtypeStruct((m, d), BF16),
        compiler_params=_cparams("parallel"),
    )(p, _pad_cols(b_gate.reshape(1, -1)), o_c, o_s, o_w)


def _nsa_layer(x, b, t, pos0, rel_bias, pool, layer, win_buf, page_table, norm_mix, w_in, b_gate, w_cmp, b_cmp, w_o):
    nq, nkv = N_HEADS * HEAD_DIM, 6 * NSA_KV_HEADS * HEAD_DIM
    p = _mm(x, _pad_cols(w_in, 2 * LANES).astype(BF16), norm_g=norm_mix)
    tq = t if pool is not None else 128
    proj = _nsa_compress(p, b, t, pos0, pool, layer, page_table, w_cmp)
    o_c, sel = _nsa_select(p, proj, b, t, pos0, rel_bias, b_cmp, tq)
    o_s = _nsa_fa('sel', p, sel, b, t, pos0, rel_bias, pool, layer, page_table, win_buf, tq)
    o_w = _nsa_fa('win', p, sel, b, t, pos0, rel_bias, pool, layer, page_table, win_buf, tq)
    o = _nsa_merge(p, b_gate, o_c, o_s, o_w)
    kv = p[:, nq:nq + nkv].reshape(b, t, 6, NSA_KV_HEADS, HEAD_DIM)
    if pool is None:
        win_out = kv[:, t - min(NSA_WINDOW, t):, 4:]
    else:
        win_out = jnp.concatenate([win_buf, kv[:, :, 4:]], axis=1)[:, -win_buf.shape[1]:]
    return _mm(o, w_o.astype(BF16), res=x), kv[:, :, :4], win_out


def _ffn_layer(x, b, t, state, norm_ffn, w_in, conv_w, conv_b, w_out):
    gu = _mm(x, w_in.astype(BF16), norm_g=norm_ffn)
    a = _ffn_gate(gu, conv_w, conv_b, t, state)
    y = _mm(a, w_out.astype(BF16), res=x)
    g_tail = gu.reshape(b, t, 2 * D_FF)[:, max(t - (CONV_W - 1), 0):, :D_FF]
    if state is None:
        prev = jnp.zeros((b, CONV_W - 1, D_FF), F32)
    else:
        prev = state
    new_state = jnp.concatenate([prev, g_tail], axis=1)[:, -(CONV_W - 1):]
    return y, new_state


def kernel(x_prompt, x_sample, state_gla, cache_nsa_kv, cache_nsa_win, cache_mla, cache_moba_kv, state_ffn_conv, page_table, rel_bias, norm_mix, norm_ffn, norm_final, ffn_w_in, ffn_conv_w, ffn_conv_b, ffn_w_out, gla_w_in, gla_b_r, gla_w_a1, gla_w_a2, gla_b_a, gla_norm, gla_w_o, nsa_w_in, nsa_b_gate, nsa_w_cmp, nsa_b_cmp, nsa_w_o, mla_w_down, mla_q_norm, mla_w_uq, mla_kv_norm, mla_w_uk, mla_w_uv, mla_w_o, moba_w_in, moba_w_o):
    bp, tp, d = x_prompt.shape
    bs, ts, _ = x_sample.shape
    past = page_table.shape[1] * PAGE_SIZE
    pos_p = jnp.arange(tp, dtype=jnp.int32)
    pos_s = past + jnp.arange(ts, dtype=jnp.int32)
    xp = x_prompt.reshape(bp * tp, d)
    xs = x_sample.reshape(bs * ts, d)
    outs = {k: [] for k in ('gla_p', 'gla_s', 'nkv_p', 'nkv_s', 'nwin_p', 'nwin_s', 'mla_p', 'mla_s', 'mkv_p', 'mkv_s',
                            'conv_p', 'conv_s')}
    for i in range(DEPTH):
        m, j = i % N_MIXERS, i // N_MIXERS
        if m == 0:
            w = (norm_mix[i], gla_w_in[j], gla_b_r[j], gla_w_a1[j], gla_w_a2[j], gla_b_a[j], gla_norm[j], gla_w_o[j])
            xp, st = _gla_layer(xp, bp, tp, None, *w)
            outs['gla_p'].append(st)
            xs, st = _gla_layer(xs, bs, ts, state_gla[j], *w)
            outs['gla_s'].append(st)
        elif m == 1:
            w = (norm_mix[i], nsa_w_in[j], nsa_b_gate[j], nsa_w_cmp[j], nsa_b_cmp[j], nsa_w_o[j])
            xp, kv, win = _nsa_layer(xp, bp, tp, 0, rel_bias, None, j, None, None, *w)
            outs['nkv_p'].append(kv)
            outs['nwin_p'].append(win)
            xs, kv, win = _nsa_layer(xs, bs, ts, past, rel_bias, cache_nsa_kv, j, cache_nsa_win[j], page_table, *w)
            outs['nkv_s'].append(kv)
            outs['nwin_s'].append(win)
        elif m == 2:
            w = (norm_mix[i], mla_w_down[j], mla_q_norm[j], mla_w_uq[j], mla_kv_norm[j], mla_w_uk[j], mla_w_uv[j],
                 mla_w_o[j])
            xp, lat = _mla_layer(xp, bp, tp, pos_p, None, j, None, *w)
            outs['mla_p'].append(lat)
            xs, lat = _mla_layer(xs, bs, ts, pos_s, cache_mla, j, page_table, *w)
            outs['mla_s'].append(lat)
        else:
            w = (norm_mix[i], moba_w_in[j], moba_w_o[j])
            xp, kv = _moba_layer(xp, bp, tp, 0, rel_bias, None, j, None, *w)
            outs['mkv_p'].append(kv)
            xs, kv = _moba_layer(xs, bs, ts, past, rel_bias, cache_moba_kv, j, page_table, *w)
            outs['mkv_s'].append(kv)
        wf = (norm_ffn[i], ffn_w_in[i], ffn_conv_w[i], ffn_conv_b[i], ffn_w_out[i])
        xp, cst = _ffn_layer(xp, bp, tp, None, *wf)
        outs['conv_p'].append(cst)
        xs, cst = _ffn_layer(xs, bs, ts, state_ffn_conv[i], *wf)
        outs['conv_s'].append(cst)
    y_prompt = _rms(xp, norm_final).reshape(bp, tp, d)
    y_sample = _rms(xs, norm_final).reshape(bs, ts, d)
    return (y_prompt, y_sample) + tuple(jnp.stack(outs[k]) for k in (
        'gla_p', 'gla_s', 'nkv_p', 'nkv_s', 'nwin_p', 'nwin_s', 'mla_p', 'mla_s', 'mkv_p', 'mkv_s', 'conv_p', 'conv_s'))
```

```python
import functools
import math

import numpy as np
import jax
import jax.numpy as jnp
from jax import lax
from jax.experimental import pallas as pl
from jax.experimental.pallas import tpu as pltpu

F32 = jnp.float32
BF16 = jnp.bfloat16

D_MODEL = 2048
DEPTH = 4
PAGE_SIZE = 128
N_MIXERS = 4
N_HEADS = 16
HEAD_DIM = D_MODEL // N_HEADS
N_BUCKETS = 32
MAX_DISTANCE = 128
RMS_EPS = 1e-6
NEG_INF = -1e30

GLA_HEADS = 4
GLA_DK = D_MODEL // 2 // GLA_HEADS
GLA_DV = D_MODEL // GLA_HEADS
GLA_GATE_RANK = 16
GLA_TAU = 16.0
GLA_CHUNK = 64

NSA_KV_HEADS = 2
NSA_CMP_BLOCK = 32
NSA_CMP_STRIDE = 16
NSA_SEL_BLOCK = 64
NSA_N_SEL = 16
NSA_WINDOW = 512
NSA_QBLOCK = 64
NSA_FORCE = 1e9

MLA_Q_RANK = 512
MLA_KV_RANK = 512
MLA_NOPE = 128
MLA_ROPE = 64
MLA_V = 128
ROPE_THETA = 10000.0

MOBA_KV_HEADS = 4
MOBA_BLOCK = 256
MOBA_TOPK = 3

D_FF = 5632
CONV_W = 3

V7X_VMEM_BYTES = 64 * 1024 * 1024
VMEM_LIMIT_BYTES = V7X_VMEM_BYTES * 3 // 4
LANES = 128
SUBLANES_BF16 = 16


def _cparams(*sem):
    return pltpu.CompilerParams(dimension_semantics=sem, vmem_limit_bytes=VMEM_LIMIT_BYTES)


def _pick(n, cap, mult):
    best = None
    for d in range(mult, min(n, cap) + 1, mult):
        if n % d == 0:
            best = d
    assert best is not None, (n, cap, mult)
    return best


def _pad_cols(w, mult=LANES):
    n = w.shape[-1]
    pad = (-n) % mult
    if pad:
        w = jnp.pad(w, [(0, 0)] * (w.ndim - 1) + [(0, pad)])
    return w


def _dot(a, b):
    return jnp.dot(a, b, preferred_element_type=F32)


def _dot_nt(a, b):
    return lax.dot_general(a, b, (((1,), (1,)), ((), ())), preferred_element_type=F32)


def _dot_tn(a, b):
    return lax.dot_general(a, b, (((0,), (0,)), ((), ())), preferred_element_type=F32)


def _sigmoid(x):
    return 1.0 / (1.0 + jnp.exp(-x))


def _mm_kernel(*refs, has_norm, has_res):
    it = iter(refs)
    x_ref, w_ref = next(it), next(it)
    g_ref = next(it) if has_norm else None
    r_ref = next(it) if has_res else None
    o_ref = next(it)
    if has_norm:
        h_ref = next(it)

        @pl.when(pl.program_id(1) == 0)
        def _():
            xf = x_ref[...].astype(F32)
            y = xf * lax.rsqrt(jnp.mean(xf * xf, axis=-1, keepdims=True) + RMS_EPS)
            h_ref[...] = (y * g_ref[...]).astype(BF16)

        lhs = h_ref[...]
    else:
        lhs = x_ref[...].astype(BF16)
    acc = _dot(lhs, w_ref[...])
    if has_res:
        acc = acc + r_ref[...]
    o_ref[...] = acc.astype(o_ref.dtype)


def _mm(x, w, *, x_col=0, norm_g=None, res=None, out_dtype=F32, tm_cap=1024, tn_cap=1024):
    m = x.shape[0]
    k, n = w.shape
    assert n % LANES == 0
    tm = _pick(m, tm_cap, SUBLANES_BF16)
    tn = n if n <= 1536 else _pick(n, tn_cap, LANES)
    if k > 4096:
        tm = _pick(m, 512, SUBLANES_BF16)
    in_specs = [pl.BlockSpec((tm, k), lambda i, j: (i, x_col)),
                pl.BlockSpec((k, tn), lambda i, j: (0, j))]
    args = [x, w]
    scratch = []
    if norm_g is not None:
        in_specs.append(pl.BlockSpec((1, k), lambda i, j: (0, 0)))
        args.append(norm_g.reshape(1, k).astype(F32))
        scratch.append(pltpu.VMEM((tm, k), BF16))
    if res is not None:
        in_specs.append(pl.BlockSpec((tm, tn), lambda i, j: (i, j)))
        args.append(res)
    return pl.pallas_call(
        functools.partial(_mm_kernel, has_norm=norm_g is not None, has_res=res is not None),
        grid=(m // tm, n // tn),
        in_specs=in_specs,
        out_specs=pl.BlockSpec((tm, tn), lambda i, j: (i, j)),
        out_shape=jax.ShapeDtypeStruct((m, n), out_dtype),
        scratch_shapes=scratch,
        compiler_params=_cparams("parallel", "arbitrary"),
    )(*args)


def _rms_kernel(x_ref, g_ref, o_ref):
    xf = x_ref[...]
    y = xf * lax.rsqrt(jnp.mean(xf * xf, axis=-1, keepdims=True) + RMS_EPS)
    o_ref[...] = y * g_ref[...]


def _rms(x, g):
    m, d = x.shape
    tm = _pick(m, 512, 8)
    return pl.pallas_call(
        _rms_kernel,
        grid=(m // tm,),
        in_specs=[pl.BlockSpec((tm, d), lambda i: (i, 0)), pl.BlockSpec((1, d), lambda i: (0, 0))],
        out_specs=pl.BlockSpec((tm, d), lambda i: (i, 0)),
        out_shape=jax.ShapeDtypeStruct((m, d), F32),
        compiler_params=_cparams("parallel"),
    )(x, g.reshape(1, d))


def _gate_kernel(*refs, seq_len, tr, has_state):
    if has_state:
        g_ref, u_ref, gp_ref, e1_ref, e2_ref, cw_ref, cb_ref, o_ref = refs
    else:
        g_ref, u_ref, gp_ref, cw_ref, cb_ref, o_ref = refs
    g = g_ref[...]
    gp = gp_ref[...]
    rl = lax.broadcasted_iota(jnp.int32, (tr, 1), 0)
    g1 = jnp.where(rl == 0, gp[7:8], pltpu.roll(g, 1, axis=0))
    g2 = jnp.where(rl == 0, gp[6:7], jnp.where(rl == 1, gp[7:8], pltpu.roll(g, 2, axis=0)))
    if seq_len >= tr:
        pos = lax.rem(pl.program_id(0) * tr, seq_len) + rl
    else:
        pos = lax.rem(rl, seq_len)
    if has_state:
        e1, e2 = e1_ref[...], e2_ref[...]
    else:
        e1 = e2 = jnp.zeros_like(g)
    g1 = jnp.where(pos >= 1, g1, e1)
    g2 = jnp.where(pos >= 2, g2, e2)
    cw = cw_ref[...]
    gc = cb_ref[...] + g2 * cw[0:1] + g1 * cw[1:2] + g * cw[2:3]
    o_ref[...] = (gc * _sigmoid(gc) * u_ref[...]).astype(o_ref.dtype)


def _ffn_gate(gu, conv_w, conv_b, seq_len, state):
    m = gu.shape[0]
    tc = _pick(D_FF, 1024, LANES)
    ncb = D_FF // tc
    tr = _pick(seq_len, 512, 8) if seq_len >= 16 else _pick(m, 512, 8 * seq_len // math.gcd(8, seq_len))
    if seq_len < tr:
        assert tr % seq_len == 0 and seq_len >= CONV_W - 1
    else:
        assert seq_len % tr == 0
    in_specs = [pl.BlockSpec((tr, tc), lambda i, j: (i, j)),
                pl.BlockSpec((tr, tc), lambda i, j: (i, j + ncb)),
                pl.BlockSpec((8, tc), lambda i, j: (jnp.maximum(i * (tr // 8) - 1, 0), j))]
    args = [gu, gu, gu]
    if state is not None:
        b = state.shape[0]
        z = jnp.zeros((b, seq_len, D_FF), F32)
        e1 = z.at[:, 0].set(state[:, 1]).reshape(m, D_FF)
        e2 = z.at[:, 0].set(state[:, 0]).at[:, 1].set(state[:, 1]).reshape(m, D_FF)
        in_specs += [pl.BlockSpec((tr, tc), lambda i, j: (i, j))] * 2
        args += [e1, e2]
    in_specs += [pl.BlockSpec((8, tc), lambda i, j: (0, j)), pl.BlockSpec((1, tc), lambda i, j: (0, j))]
    args += [jnp.pad(conv_w, ((0, 8 - CONV_W), (0, 0))), conv_b.reshape(1, D_FF)]
    return pl.pallas_call(
        functools.partial(_gate_kernel, seq_len=seq_len, tr=tr, has_state=state is not None),
        grid=(m // tr, ncb),
        in_specs=in_specs,
        out_specs=pl.BlockSpec((tr, tc), lambda i, j: (i, j)),
        out_shape=jax.ShapeDtypeStruct((m, D_FF), BF16),
        compiler_params=_cparams("parallel", "parallel"),
    )(*args)


def _cumsum_rows(x):
    c = x.shape[0]
    row = lax.broadcasted_iota(jnp.int32, x.shape, 0)
    s = 1
    while s < c:
        x = x + jnp.where(row >= s, pltpu.roll(x, s, axis=0), 0.0)
        s *= 2
    return x


def _gla_kernel(*refs, c, has_state):
    it = iter(refs)
    q_ref, k_ref, v_ref, r_ref, ga_ref, wa2_ref, ba_ref, ng_ref, br_ref = (next(it) for _ in range(9))
    s0_ref = next(it) if has_state else None
    o_ref, sout_ref, st_ref = next(it), next(it), next(it)
    ci = pl.program_id(2)

    @pl.when(ci == 0)
    def _():
        if has_state:
            st_ref[...] = s0_ref[...].T
        else:
            st_ref[...] = jnp.zeros_like(st_ref)

    q = q_ref[...] * (GLA_DK ** -0.5)
    k = k_ref[...]
    v16 = v_ref[...].astype(BF16)
    lr = _dot(ga_ref[...].astype(BF16), wa2_ref[...]) + ba_ref[...]
    log_a = (jnp.minimum(lr, 0.0) - jnp.log(1.0 + jnp.exp(-jnp.abs(lr)))) / GLA_TAU
    cum = _cumsum_rows(log_a)
    q_dec = (q * jnp.exp(cum)).astype(BF16)
    k_inv = (k * jnp.exp(-cum)).astype(BF16)
    att = _dot_nt(q_dec, k_inv)
    ri = lax.broadcasted_iota(jnp.int32, (c, c), 0)
    cj = lax.broadcasted_iota(jnp.int32, (c, c), 1)
    att = jnp.where(ri >= cj, att, 0.0)
    st = st_ref[...]
    o = _dot(att.astype(BF16), v16) + _dot_nt(q_dec, st.astype(BF16))
    last = cum[c - 1:c]
    k_dec = (k * jnp.exp(last - cum)).astype(BF16)
    st_new = st * jnp.exp(last) + _dot_tn(v16, k_dec)
    st_ref[...] = st_new
    on = o * lax.rsqrt(jnp.mean(o * o, axis=-1, keepdims=True) + RMS_EPS) * ng_ref[...]
    rr = r_ref[...] + br_ref[...]
    o_ref[...] = (on * (rr * _sigmoid(rr))).astype(o_ref.dtype)

    @pl.when(ci == pl.num_programs(2) - 1)
    def _():
        sout_ref[...] = st_new.T


def _gla_scan(p, b, t, w_a2p, b_a, norm_g, b_r, s0):
    c = GLA_CHUNK if t % GLA_CHUNK == 0 else t
    nch = t // c
    h = GLA_HEADS
    nk = h * GLA_DK
    kq, kv_ = GLA_DK, GLA_DV
    row = lambda bi, hi, ci: bi * nch + ci
    in_specs = [
        pl.BlockSpec((c, kq), lambda bi, hi, ci: (row(bi, hi, ci), hi)),
        pl.BlockSpec((c, kq), lambda bi, hi, ci: (row(bi, hi, ci), h + hi)),
        pl.BlockSpec((c, kv_), lambda bi, hi, ci: (row(bi, hi, ci), 2 * nk // kv_ + hi)),
        pl.BlockSpec((c, kv_), lambda bi, hi, ci: (row(bi, hi, ci), 2 * nk // kv_ + h + hi)),
        pl.BlockSpec((c, LANES), lambda bi, hi, ci: (row(bi, hi, ci), (2 * nk + 2 * h * kv_) // LANES)),
        pl.BlockSpec((LANES, kq), lambda bi, hi, ci: (0, hi)),
        pl.BlockSpec((1, kq), lambda bi, hi, ci: (0, hi)),
        pl.BlockSpec((1, kv_), lambda bi, hi, ci: (0, 0)),
        pl.BlockSpec((1, kv_), lambda bi, hi, ci: (0, hi)),
    ]
    args = [p, p, p, p, p, w_a2p, b_a.reshape(1, nk), norm_g.reshape(1, kv_), b_r.reshape(1, h * kv_)]
    if s0 is not None:
        in_specs.append(pl.BlockSpec((None, None, kq, kv_), lambda bi, hi, ci: (bi, hi, 0, 0)))
        args.append(s0)
    return pl.pallas_call(
        functools.partial(_gla_kernel, c=c, has_state=s0 is not None),
        grid=(b, h, nch),
        in_specs=in_specs,
        out_specs=[pl.BlockSpec((c, kv_), lambda bi, hi, ci: (row(bi, hi, ci), hi)),
                   pl.BlockSpec((None, None, kq, kv_), lambda bi, hi, ci: (bi, hi, 0, 0))],
        out_shape=[jax.ShapeDtypeStruct((b * t, h * kv_), BF16),
                   jax.ShapeDtypeStruct((b, h, kq, kv_), F32)],
        scratch_shapes=[pltpu.VMEM((kv_, kq), F32)],
        compiler_params=_cparams("parallel", "parallel", "arbitrary"),
    )(*args)


def _gla_layer(x, b, t, s0, norm_mix, w_in, b_r, w_a1, w_a2, b_a, norm_g, w_o):
    w_cat = jnp.concatenate([w_in, _pad_cols(w_a1)], axis=1).astype(BF16)
    p = _mm(x, w_cat, norm_g=norm_mix)
    w_a2p = jnp.pad(w_a2, ((0, LANES - GLA_GATE_RANK), (0, 0))).astype(BF16)
    o, s_new = _gla_scan(p, b, t, w_a2p, b_a, norm_g, b_r, s0)
    return _mm(o, w_o.astype(BF16), res=x), s_new


def _rel_bucket_np(dist):
    n = np.maximum(dist, 0)
    exact = N_BUCKETS // 2
    lg = np.log(np.maximum(n, 1).astype(np.float32) / np.float32(exact)) / np.float32(math.log(MAX_DISTANCE / exact))
    large = np.minimum(exact + (lg * np.float32(N_BUCKETS - exact)).astype(np.int32), N_BUCKETS - 1)
    return np.where(n < exact, n, large)


def rel_bucket(dist):
    n = jnp.maximum(dist, 0)
    exact = N_BUCKETS // 2
    lg = jnp.log(jnp.maximum(n, 1).astype(jnp.float32) / exact) / math.log(MAX_DISTANCE / exact)
    large = jnp.minimum(exact + (lg * (N_BUCKETS - exact)).astype(jnp.int32), N_BUCKETS - 1)
    return jnp.where(n < exact, n, large)


def _bias_lookup(rel_bias, d):
    onehot = jax.nn.one_hot(rel_bucket(d), N_BUCKETS, dtype=F32)
    return jnp.einsum('...b,bh->...h', onehot, rel_bias.astype(F32), precision=lax.Precision.HIGHEST)


def _toeplitz_bias(rel_bias, tq, tk, c):
    d = c + jnp.arange(tq, dtype=jnp.int32)[:, None] - jnp.arange(tk, dtype=jnp.int32)[None, :]
    return _bias_lookup(rel_bias, d).transpose(2, 0, 1).reshape(N_HEADS * tq, tk)


def _rope_tables(pos):
    half = MLA_ROPE // 2
    inv = ROPE_THETA ** (-jnp.arange(half, dtype=jnp.float32) / half)
    ang = pos.astype(jnp.float32)[:, None] * inv
    cos, sin = jnp.cos(ang), jnp.sin(ang)
    z = jnp.zeros_like(cos)
    zz = jnp.zeros((pos.shape[0], LANES - MLA_ROPE), F32)
    c = jnp.concatenate([cos, cos, zz], axis=1)
    sa = jnp.concatenate([-sin, z, zz], axis=1)
    sb = jnp.concatenate([z, sin, zz], axis=1)
    return c, sa, sb


def _rope128(x, c, sa, sb):
    return x * c + pltpu.roll(x, LANES - MLA_ROPE // 2, axis=1) * sa + pltpu.roll(x, MLA_ROPE // 2, axis=1) * sb


def _mla_latent_kernel(ckv_ref, kpe_ref, g_ref, c_ref, sa_ref, sb_ref, o_ref):
    x = ckv_ref[...]
    y = x * lax.rsqrt(jnp.mean(x * x, axis=-1, keepdims=True) + RMS_EPS) * g_ref[...]
    kp = _rope128(kpe_ref[...], c_ref[...], sa_ref[...], sb_ref[...])
    o_ref[:, 0:MLA_KV_RANK] = y
    o_ref[:, MLA_KV_RANK:MLA_KV_RANK + MLA_ROPE] = kp[:, 0:MLA_ROPE]


def _mla_q_kernel(xn_ref, xp_ref, w_ref, c_ref, sa_ref, sb_ref, qa_ref, qp_ref):
    qa_ref[...] = _dot(xn_ref[...].astype(BF16), w_ref[...]).astype(qa_ref.dtype)
    qp_ref[...] = _rope128(xp_ref[...], c_ref[...], sa_ref[...], sb_ref[...]).astype(qp_ref.dtype)


def _mla_project(x, t, pos, norm_mix, w_down, q_norm, w_uq, kv_norm, w_uk):
    m = x.shape[0]
    h = N_HEADS
    d = _mm(x, _pad_cols(w_down).astype(BF16), norm_g=norm_mix)
    w3 = w_uq.reshape(MLA_Q_RANK, h, MLA_NOPE + MLA_ROPE)
    w_n = w3[:, :, :MLA_NOPE].reshape(MLA_Q_RANK, h * MLA_NOPE)
    w_p = jnp.pad(w3[:, :, MLA_NOPE:], ((0, 0), (0, 0), (0, LANES - MLA_ROPE))).reshape(MLA_Q_RANK, h * LANES)
    q = _mm(d, jnp.concatenate([w_n, w_p], axis=1).astype(BF16), norm_g=q_norm)
    tabs = _rope_tables(pos)
    period = pos.shape[0]
    tm = _pick(m, 512, SUBLANES_BF16)
    if period < tm:
        assert tm % period == 0
        tabs = tuple(jnp.tile(tb, (tm // period, 1)) for tb in tabs)
        ntab = 1
    else:
        assert period % tm == 0
        ntab = period // tm
    latent = pl.pallas_call(
        _mla_latent_kernel,
        grid=(m // tm,),
        in_specs=[pl.BlockSpec((tm, MLA_KV_RANK), lambda i: (i, 1)),
                  pl.BlockSpec((tm, LANES), lambda i: (i, (MLA_Q_RANK + MLA_KV_RANK) // LANES)),
                  pl.BlockSpec((1, MLA_KV_RANK), lambda i: (0, 0))]
        + [pl.BlockSpec((tm, LANES), lambda i: (i % ntab, 0))] * 3,
        out_specs=pl.BlockSpec((tm, MLA_KV_RANK + MLA_ROPE), lambda i: (i, 0)),
        out_shape=jax.ShapeDtypeStruct((m, MLA_KV_RANK + MLA_ROPE), F32),
        compiler_params=_cparams("parallel"),
    )(d, d, kv_norm.reshape(1, MLA_KV_RANK), *tabs)
    w_ukt = w_uk.transpose(1, 2, 0).astype(BF16)
    qa, qp = pl.pallas_call(
        _mla_q_kernel,
        grid=(m // tm, h),
        in_specs=[pl.BlockSpec((tm, MLA_NOPE), lambda i, hi: (i, hi)),
                  pl.BlockSpec((tm, LANES), lambda i, hi: (i, h + hi)),
                  pl.BlockSpec((None, MLA_NOPE, MLA_KV_RANK), lambda i, hi: (hi, 0, 0))]
        + [pl.BlockSpec((tm, LANES), lambda i, hi: (i % ntab, 0))] * 3,
        out_specs=[pl.BlockSpec((None, tm, MLA_KV_RANK), lambda i, hi: (hi, i, 0)),
                   pl.BlockSpec((None, tm, LANES), lambda i, hi: (hi, i, 0))],
        out_shape=[jax.ShapeDtypeStruct((h, m, MLA_KV_RANK), BF16),
                   jax.ShapeDtypeStruct((h, m, LANES), BF16)],
        compiler_params=_cparams("parallel", "parallel"),
    )(q, q, w_ukt, *tabs)
    return qa, qp, latent


def _softmax_step(s, mask, m_ref, l_ref):
    z = jnp.where(mask, s, NEG_INF) if mask is not None else s
    m_old = m_ref[...]
    m_new = jnp.maximum(m_old, jnp.max(z, axis=-1, keepdims=True))
    p = jnp.exp(z - m_new)
    if mask is not None:
        p = jnp.where(mask, p, 0.0)
    alpha = jnp.exp(m_old - m_new)
    l_ref[...] = alpha * l_ref[...] + jnp.sum(p, axis=-1, keepdims=True)
    m_ref[...] = m_new
    return alpha, p


def _mla_finish(acc_ref, l_ref, wuv_ref, o_ref, tq):
    o_lat = (acc_ref[...] / jnp.maximum(l_ref[...], 1e-30)).astype(BF16)
    for hi in range(N_HEADS):
        o_ref[:, hi * MLA_V:(hi + 1) * MLA_V] = _dot(o_lat[hi * tq:(hi + 1) * tq], wuv_ref[hi]).astype(o_ref.dtype)


MLA_SCALE = (MLA_NOPE + MLA_ROPE) ** -0.5


def _mla_prompt_kernel(qa_ref, qp_ref, lat_ref, wuv_ref, o_ref, m_ref, l_ref, acc_ref, *, tq, tk):
    qt, kt = pl.program_id(1), pl.program_id(2)
    rows = N_HEADS * tq

    @pl.when(kt == 0)
    def _():
        m_ref[...] = jnp.full_like(m_ref, NEG_INF)
        l_ref[...] = jnp.zeros_like(l_ref)
        acc_ref[...] = jnp.zeros_like(acc_ref)

    def tile(masked):
        kc = lat_ref[:, 0:MLA_KV_RANK].astype(BF16)
        kp = lat_ref[:, MLA_KV_RANK:MLA_KV_RANK + MLA_ROPE].astype(BF16)
        qa = qa_ref[...].reshape(rows, MLA_KV_RANK)
        qp = qp_ref[...].reshape(rows, LANES)[:, 0:MLA_ROPE]
        s = (_dot_nt(qa, kc) + _dot_nt(qp, kp)) * MLA_SCALE
        mask = None
        if masked:
            ti = qt * tq + lax.rem(lax.broadcasted_iota(jnp.int32, (rows, tk), 0), tq)
            kj = kt * tk + lax.broadcasted_iota(jnp.int32, (rows, tk), 1)
            mask = kj <= ti
        alpha, p = _softmax_step(s, mask, m_ref, l_ref)
        acc_ref[...] = alpha * acc_ref[...] + _dot(p.astype(BF16), kc)

    interior = kt * tk + tk - 1 <= qt * tq

    @pl.when(interior)
    def _():
        tile(False)

    @pl.when(jnp.logical_and(jnp.logical_not(interior), kt * tk <= qt * tq + tq - 1))
    def _():
        tile(True)

    @pl.when(kt == pl.num_programs(2) - 1)
    def _():
        _mla_finish(acc_ref, l_ref, wuv_ref, o_ref, tq)


def _mla_prompt_attend(qa, qp, latent, b, t, w_uv16):
    h = N_HEADS
    tq = _pick(t, 128, SUBLANES_BF16)
    tk = _pick(t, 512, 8)
    nq, nk = t // tq, t // tk
    rows = h * tq

    def kmap(bi, qt, kt):
        return (bi * nk + jnp.minimum(kt, (qt * tq + tq - 1) // tk), 0)

    return pl.pallas_call(
        functools.partial(_mla_prompt_kernel, tq=tq, tk=tk),
        grid=(b, nq, nk),
        in_specs=[pl.BlockSpec((h, tq, MLA_KV_RANK), lambda bi, qt, kt: (0, bi * nq + qt, 0)),
                  pl.BlockSpec((h, tq, LANES), lambda bi, qt, kt: (0, bi * nq + qt, 0)),
                  pl.BlockSpec((tk, MLA_KV_RANK + MLA_ROPE), kmap),
                  pl.BlockSpec((h, MLA_KV_RANK, MLA_V), lambda bi, qt, kt: (0, 0, 0))],
        out_specs=pl.BlockSpec((tq, h * MLA_V), lambda bi, qt, kt: (bi * nq + qt, 0)),
        out_shape=jax.ShapeDtypeStruct((b * t, h * MLA_V), BF16),
        scratch_shapes=[pltpu.VMEM((rows, 1), F32), pltpu.VMEM((rows, 1), F32), pltpu.VMEM((rows, MLA_KV_RANK), F32)],
        compiler_params=_cparams("parallel", "parallel", "arbitrary"),
    )(qa, qp, latent, w_uv16)


def _mla_sample_kernel(pt_ref, qa_ref, qp_ref, *rest, npp, ts):
    page_refs = rest[:npp]
    new_ref, wuv_ref, o_ref, m_ref, l_ref, acc_ref = rest[npp:]
    st = pl.program_id(1)
    rows = N_HEADS * ts

    @pl.when(st == 0)
    def _():
        m_ref[...] = jnp.full_like(m_ref, NEG_INF)
        l_ref[...] = jnp.zeros_like(l_ref)
        acc_ref[...] = jnp.zeros_like(acc_ref)

    qa = qa_ref[...]
    qp = qp_ref[:, 0:MLA_ROPE]
    kcs = [r[0:MLA_KV_RANK, :].astype(BF16) for r in page_refs]
    kps = [r[MLA_KV_RANK:MLA_KV_RANK + MLA_ROPE, :].astype(BF16) for r in page_refs]
    s = jnp.concatenate([_dot(qa, kc) + _dot(qp, kp) for kc, kp in zip(kcs, kps)], axis=1) * MLA_SCALE
    alpha, p = _softmax_step(s, None, m_ref, l_ref)
    p16 = p.astype(BF16)
    pv = _dot_nt(p16[:, 0:PAGE_SIZE], kcs[0])
    for j in range(1, npp):
        pv = pv + _dot_nt(p16[:, j * PAGE_SIZE:(j + 1) * PAGE_SIZE], kcs[j])
    acc_ref[...] = alpha * acc_ref[...] + pv

    @pl.when(st == pl.num_programs(1) - 1)
    def _():
        kc = new_ref[:, 0:MLA_KV_RANK].astype(BF16)
        kp = new_ref[:, MLA_KV_RANK:MLA_KV_RANK + MLA_ROPE].astype(BF16)
        s2 = (_dot_nt(qa, kc) + _dot_nt(qp, kp)) * MLA_SCALE
        ti = lax.rem(lax.broadcasted_iota(jnp.int32, (rows, ts), 0), ts)
        kj = lax.broadcasted_iota(jnp.int32, (rows, ts), 1)
        alpha2, p2 = _softmax_step(s2, kj <= ti, m_ref, l_ref)
        acc_ref[...] = alpha2 * acc_ref[...] + _dot(p2.astype(BF16), kc)
        _mla_finish(acc_ref, l_ref, wuv_ref, o_ref, ts)


def _mla_sample_attend(qa, qp, latent, cache, layer, page_table, b, ts, w_uv16):
    h = N_HEADS
    npg = page_table.shape[1]
    npp = _pick(npg, 16, 1)
    nst = npg // npp
    rows = h * ts
    qa_s = qa.reshape(h, b, ts, MLA_KV_RANK).transpose(1, 0, 2, 3).reshape(b, rows, MLA_KV_RANK)
    qp_s = qp.reshape(h, b, ts, LANES).transpose(1, 0, 2, 3).reshape(b, rows, LANES)
    lat3 = latent.reshape(b, ts, MLA_KV_RANK + MLA_ROPE)
    cache_t = cache.transpose(0, 1, 3, 2)

    def page_map(j):
        return lambda bi, st, pt: (layer, pt[bi * npg + st * npp + j], 0, 0)

    grid_spec = pltpu.PrefetchScalarGridSpec(
        num_scalar_prefetch=1,
        grid=(b, nst),
        in_specs=[pl.BlockSpec((None, rows, MLA_KV_RANK), lambda bi, st, pt: (bi, 0, 0)),
                  pl.BlockSpec((None, rows, LANES), lambda bi, st, pt: (bi, 0, 0))]
        + [pl.BlockSpec((None, None, MLA_KV_RANK + MLA_ROPE, PAGE_SIZE), page_map(j)) for j in range(npp)]
        + [pl.BlockSpec((None, ts, MLA_KV_RANK + MLA_ROPE), lambda bi, st, pt: (bi, 0, 0)),
           pl.BlockSpec((h, MLA_KV_RANK, MLA_V), lambda bi, st, pt: (0, 0, 0))],
        out_specs=pl.BlockSpec((None, ts, h * MLA_V), lambda bi, st, pt: (bi, 0, 0)),
        scratch_shapes=[pltpu.VMEM((rows, 1), F32), pltpu.VMEM((rows, 1), F32), pltpu.VMEM((rows, MLA_KV_RANK), F32)],
    )
    out = pl.pallas_call(
        functools.partial(_mla_sample_kernel, npp=npp, ts=ts),
        grid_spec=grid_spec,
        out_shape=jax.ShapeDtypeStruct((b, ts, h * MLA_V), BF16),
        compiler_params=_cparams("parallel", "arbitrary"),
    )(page_table.reshape(-1), qa_s, qp_s, *([cache_t] * npp), lat3, w_uv16)
    return out.reshape(b * ts, h * MLA_V)


def _mla_layer(x, b, t, pos, cache, layer, page_table, norm_mix, w_down, q_norm, w_uq, kv_norm, w_uk, w_uv, w_o):
    qa, qp, latent = _mla_project(x, t, pos, norm_mix, w_down, q_norm, w_uq, kv_norm, w_uk)
    w_uv16 = w_uv.transpose(1, 0, 2).astype(BF16)
    if cache is None:
        o = _mla_prompt_attend(qa, qp, latent, b, t, w_uv16)
    else:
        o = _mla_sample_attend(qa, qp, latent, cache, layer, page_table, b, t, w_uv16)
    return _mm(o, w_o.astype(BF16), res=x), latent.reshape(b, t, MLA_KV_RANK + MLA_ROPE)


MOBA_SCALE = HEAD_DIM ** -0.5
MOBA_HPG = N_HEADS // MOBA_KV_HEADS


def _group_rows(q_ref, g, hpg):
    return jnp.concatenate([q_ref[:, (g * hpg + j) * HEAD_DIM:(g * hpg + j + 1) * HEAD_DIM] for j in range(hpg)],
                           axis=0)


def _moba_kernel(pt_ref, q_ref, *rest, tq, pos0, n_past, paged, own_keys, bps):
    ppb = MOBA_BLOCK // PAGE_SIZE
    nkv = ppb * bps if paged else 2
    kv_refs = rest[:nkv]
    own_refs = rest[nkv:nkv + 2]
    bp_ref, bo_ref, o_ref, m_all, l_all, g_all, o_all = rest[nkv + 2:]
    qt, n = pl.program_id(1), pl.program_id(2)
    hpg, grp = MOBA_HPG, MOBA_KV_HEADS
    gw = grp * HEAD_DIM
    rg = hpg * tq
    t0 = pos0 + qt * tq
    own = t0 // MOBA_BLOCK

    @pl.when(n == 0)
    def _():
        m_all[...] = jnp.full_like(m_all, NEG_INF)
        g_all[...] = jnp.full_like(g_all, NEG_INF)
        l_all[...] = jnp.zeros_like(l_all)
        o_all[...] = jnp.zeros_like(o_all)

    @pl.when(jnp.logical_and(n < n_past // bps, n * bps < own))
    def _():
        lane = lax.broadcasted_iota(jnp.int32, (rg, LANES), 1)
        for g in range(grp):
            rows = slice(g * rg, (g + 1) * rg)
            qg = _group_rows(q_ref, g, hpg)
            q2 = jnp.concatenate([qg, qg * MOBA_SCALE], axis=0).astype(BF16)
            if paged:
                kg = jnp.concatenate([r[pl.ds(g, PAGE_SIZE, stride=2 * grp), :] for r in kv_refs], axis=0)
                vg = jnp.concatenate([r[pl.ds(grp + g, PAGE_SIZE, stride=2 * grp), :] for r in kv_refs], axis=0)
            else:
                kg = kv_refs[0][:, g * HEAD_DIM:(g + 1) * HEAD_DIM]
                vg = kv_refs[1][:, g * HEAD_DIM:(g + 1) * HEAD_DIM]
            kg, vg = kg.astype(BF16), vg.astype(BF16)
            both = _dot_nt(q2, kg)
            m_new, l_new, g_new = m_all[rows, :], l_all[rows, :], g_all[rows, :]
            for i in range(bps):
                blk = n * bps + i
                cols = slice(i * MOBA_BLOCK, (i + 1) * MOBA_BLOCK)
                bidx = jnp.where(blk == n_past - 1, 1, 0) if paged else 0
                gate = jnp.sum(both[0:rg, cols], axis=-1, keepdims=True) * (1.0 / MOBA_BLOCK)
                s = both[rg:2 * rg, cols] + bp_ref[bidx, rows, :]
                mx = jnp.max(s, axis=-1, keepdims=True)
                p = jnp.exp(s - mx)
                hit = lane == blk
                m_new = jnp.where(hit, mx, m_new)
                l_new = jnp.where(hit, jnp.sum(p, axis=-1, keepdims=True), l_new)
                g_new = jnp.where(hit, gate, g_new)
                o_all[blk, rows, :] = _dot(p.astype(BF16), vg[i * MOBA_BLOCK:(i + 1) * MOBA_BLOCK])
            m_all[rows, :] = m_new
            l_all[rows, :] = l_new
            g_all[rows, :] = g_new

    @pl.when(n == pl.num_programs(2) - 1)
    def _():
        ko, vo = own_refs[0][...], own_refs[1][...]
        nko = bo_ref.shape[1]
        if own_keys < nko:
            zpad = jnp.zeros((nko - own_keys, gw), F32)
            ko = jnp.concatenate([ko, zpad], axis=0)
            vo = jnp.concatenate([vo, zpad], axis=0)
        ti = t0 + lax.rem(lax.broadcasted_iota(jnp.int32, (rg, nko), 0), tq)
        kj = own * MOBA_BLOCK + lax.broadcasted_iota(jnp.int32, (rg, nko), 1)
        mask = kj <= ti
        lane = lax.broadcasted_iota(jnp.int32, (rg, LANES), 1)
        for g in range(grp):
            rows = slice(g * rg, (g + 1) * rg)
            qg = _group_rows(q_ref, g, hpg)
            kg = ko[:, g * HEAD_DIM:(g + 1) * HEAD_DIM].astype(BF16)
            vg = vo[:, g * HEAD_DIM:(g + 1) * HEAD_DIM].astype(BF16)
            s = _dot_nt((qg * MOBA_SCALE).astype(BF16), kg) + bo_ref[rows, :]
            z = jnp.where(mask, s, NEG_INF)
            m_o = jnp.max(z, axis=-1, keepdims=True)
            p = jnp.where(mask, jnp.exp(z - m_o), 0.0)
            l_o = jnp.sum(p, axis=-1, keepdims=True)
            o_o = _dot(p.astype(BF16), vg)
            gg = g_all[rows, :]
            sel = jnp.zeros((rg, LANES), jnp.bool_)
            for _ in range(MOBA_TOPK):
                mx = jnp.max(gg, axis=-1, keepdims=True)
                idx = jnp.min(jnp.where(gg == mx, lane, LANES), axis=-1, keepdims=True)
                hit = lane == idx
                sel = jnp.logical_or(sel, jnp.logical_and(hit, mx > NEG_INF / 2))
                gg = jnp.where(hit, -3e38, gg)
            mm = m_all[rows, :]
            m_tot = jnp.maximum(m_o, jnp.max(jnp.where(sel, mm, NEG_INF), axis=-1, keepdims=True))
            w = jnp.where(sel, jnp.exp(mm - m_tot), 0.0)
            w_o = jnp.exp(m_o - m_tot)
            l_tot = jnp.sum(w * l_all[rows, :], axis=-1, keepdims=True) + w_o * l_o
            o = w_o * o_o
            for nb in range(n_past):
                o = o + w[:, nb:nb + 1] * o_all[nb, rows, :]
            o = o / jnp.maximum(l_tot, 1e-30)
            for j in range(hpg):
                hh = g * hpg + j
                o_ref[:, hh * HEAD_DIM:(hh + 1) * HEAD_DIM] = o[j * tq:(j + 1) * tq].astype(o_ref.dtype)


def _moba_attend(p, b, t, pos0, rel_bias, cache, layer, page_table):
    h, grp = N_HEADS, MOBA_KV_HEADS
    gw = grp * HEAD_DIM
    paged = cache is not None
    assert pos0 % MOBA_BLOCK == 0
    if paged:
        assert t <= MOBA_BLOCK
        tq, nq = t, 1
        n_past = pos0 // MOBA_BLOCK
        own_keys = t
        nko = LANES
        ppb = MOBA_BLOCK // PAGE_SIZE
        npg = page_table.shape[1]
        pt = page_table.reshape(-1)
        cache4 = cache.reshape(cache.shape[0], cache.shape[1], PAGE_SIZE * 2 * grp, HEAD_DIM)

        bps = _pick(n_past, 4, 1)
        n_steps = n_past // bps

        def page_map(j):
            return lambda bi, qt, n, pt_: (layer, pt_[bi * npg + jnp.minimum(n, n_steps - 1) * bps * ppb + j], 0, 0)

        kv_specs = [pl.BlockSpec((None, None, PAGE_SIZE * 2 * grp, HEAD_DIM), page_map(j)) for j in range(bps * ppb)]
        kv_args = [cache4] * (bps * ppb)
        own_specs = [pl.BlockSpec((t, gw), lambda bi, qt, n, pt_: (bi, h * HEAD_DIM // gw)),
                     pl.BlockSpec((t, gw), lambda bi, qt, n, pt_: (bi, h * HEAD_DIM // gw + 1))]
        bias_past = jnp.stack([jnp.broadcast_to(jnp.repeat(rel_bias[N_BUCKETS - 1], tq)[:, None], (h * tq, MOBA_BLOCK)),
                               _toeplitz_bias(rel_bias, tq, MOBA_BLOCK, MOBA_BLOCK)])
        bias_own = _toeplitz_bias(rel_bias, tq, nko, 0)[None]
        bp_spec = pl.BlockSpec((2, h * tq, MOBA_BLOCK), lambda bi, qt, n, pt_: (0, 0, 0))

        def bo_map(bi, qt, n, pt_):
            return (0, 0, 0)
    else:
        assert t % MOBA_BLOCK == 0
        tq = 128
        nq = t // tq
        nblk = t // MOBA_BLOCK
        n_past = nblk - 1
        own_keys = MOBA_BLOCK
        nko = MOBA_BLOCK
        pt = jnp.zeros((1,), jnp.int32)
        qpb = MOBA_BLOCK // tq

        def past_map(col):
            return lambda bi, qt, n, pt_: (bi * nblk + jnp.minimum(n, jnp.maximum(qt // qpb - 1, 0)), col)

        kcol = h * HEAD_DIM // gw
        kv_specs = [pl.BlockSpec((MOBA_BLOCK, gw), past_map(kcol)), pl.BlockSpec((MOBA_BLOCK, gw), past_map(kcol + 1))]
        kv_args = [p, p]
        own_specs = [pl.BlockSpec((MOBA_BLOCK, gw), lambda bi, qt, n, pt_: (bi * nblk + qt // qpb, kcol)),
                     pl.BlockSpec((MOBA_BLOCK, gw), lambda bi, qt, n, pt_: (bi * nblk + qt // qpb, kcol + 1))]
        far = jnp.broadcast_to(jnp.repeat(rel_bias[N_BUCKETS - 1], tq)[:, None], (h * tq, MOBA_BLOCK))
        bias_past = jnp.stack([far] + [_toeplitz_bias(rel_bias, tq, MOBA_BLOCK, MOBA_BLOCK + r * tq) for r in range(qpb)])
        bias_own = jnp.stack([_toeplitz_bias(rel_bias, tq, MOBA_BLOCK, r * tq) for r in range(qpb)])

        bps, n_steps = 1, n_past
        bp_spec = pl.BlockSpec((1, h * tq, MOBA_BLOCK),
                               lambda bi, qt, n, pt_: (jnp.where(n == qt // qpb - 1, 1 + qt % qpb, 0), 0, 0))

        def bo_map(bi, qt, n, pt_):
            return (qt % qpb, 0, 0)

    rows = h * tq
    grid_spec = pltpu.PrefetchScalarGridSpec(
        num_scalar_prefetch=1,
        grid=(b, nq, n_steps + 1),
        in_specs=[pl.BlockSpec((tq, h * HEAD_DIM), lambda bi, qt, n, pt_: (bi * nq + qt, 0))]
        + kv_specs + own_specs
        + [bp_spec, pl.BlockSpec((None, rows, nko), bo_map)],
        out_specs=pl.BlockSpec((tq, h * HEAD_DIM), lambda bi, qt, n, pt_: (bi * nq + qt, 0)),
        scratch_shapes=[pltpu.VMEM((rows, LANES), F32), pltpu.VMEM((rows, LANES), F32), pltpu.VMEM((rows, LANES), F32),
                        pltpu.VMEM((max(n_past, 1), rows, HEAD_DIM), F32)],
    )
    assert n_past <= LANES
    return pl.pallas_call(
        functools.partial(_moba_kernel, tq=tq, pos0=pos0, n_past=n_past, paged=paged, own_keys=own_keys, bps=bps),
        grid_spec=grid_spec,
        out_shape=jax.ShapeDtypeStruct((b * t, h * HEAD_DIM), BF16),
        compiler_params=_cparams("parallel", "parallel", "arbitrary"),
    )(pt, p, *kv_args, p, p, bias_past, bias_own)


def _moba_layer(x, b, t, pos0, rel_bias, cache, layer, page_table, norm_mix, w_in, w_o):
    p = _mm(x, w_in.astype(BF16), norm_g=norm_mix)
    o = _moba_attend(p, b, t, pos0, rel_bias, cache, layer, page_table)
    kv = p[:, N_HEADS * HEAD_DIM:].reshape(b, t, 2, MOBA_KV_HEADS, HEAD_DIM)
    return _mm(o, w_o.astype(BF16), res=x), kv


NSA_HPG = N_HEADS // NSA_KV_HEADS
NSA_SCALE = HEAD_DIM ** -0.5
NSA_KV_COL0 = N_HEADS * HEAD_DIM // LANES
NSA_CACHE_ROWS = 4 * NSA_KV_HEADS
NSA_WIN_ROWS = 2 * NSA_KV_HEADS
NSA_PARTS = NSA_CMP_BLOCK // NSA_CMP_STRIDE
NSA_SEL_SHIFT = NSA_SEL_BLOCK.bit_length() - 1
assert 1 << NSA_SEL_SHIFT == NSA_SEL_BLOCK and NSA_PARTS == 2


def _dot3(a, b16):
    a1 = a.astype(BF16)
    r1 = a - a1.astype(F32)
    a2 = r1.astype(BF16)
    a3 = (r1 - a2.astype(F32)).astype(BF16)
    return _dot(a1, b16) + _dot(a2, b16) + _dot(a3, b16)


def _far_bias(rel_bias, tq, kb):
    return jnp.broadcast_to(jnp.repeat(rel_bias[N_BUCKETS - 1].astype(F32), tq)[:, None], (N_HEADS * tq, kb))


def _nsa_cmp_kernel(pt_ref, *refs, n_src, paged, cps):
    srcs = refs[:n_src]
    w_ref, o_ref = refs[n_src:]
    for c in range(2 * NSA_KV_HEADS):
        acc = jnp.zeros((cps, 2 * HEAD_DIM), F32)
        for s in range(NSA_CMP_STRIDE):
            if paged:
                xs = jnp.concatenate(
                    [r[pl.ds(s * NSA_CACHE_ROWS + c, PAGE_SIZE // NSA_CMP_STRIDE, stride=NSA_CMP_STRIDE * NSA_CACHE_ROWS), :]
                     for r in srcs], axis=0)
            else:
                xs = srcs[c][pl.ds(s, cps, stride=NSA_CMP_STRIDE), :]
            acc = acc + _dot(xs.astype(BF16), w_ref[c // NSA_KV_HEADS, s])
        o_ref[c] = acc


def _nsa_compress(p, b, t, pos0, pool, layer, page_table, w_cmp):
    ncomb = 2 * NSA_KV_HEADS
    wc = w_cmp.reshape(2, NSA_PARTS, NSA_CMP_STRIDE, HEAD_DIM, HEAD_DIM).transpose(0, 2, 3, 1, 4)
    wc = wc.reshape(2, NSA_CMP_STRIDE, HEAD_DIM, NSA_PARTS * HEAD_DIM).astype(BF16)
    w_spec = pl.BlockSpec(wc.shape, lambda bi, st, pt_: (0, 0, 0, 0))
    if pool is None:
        assert t % NSA_CMP_STRIDE == 0
        n_chunk = t // NSA_CMP_STRIDE
        cps, nst = n_chunk, 1
        pt = jnp.zeros((1,), jnp.int32)
        srcs = [p] * ncomb
        src_specs = [pl.BlockSpec((t, HEAD_DIM), (lambda c: lambda bi, st, pt_: (bi, NSA_KV_COL0 + c))(c))
                     for c in range(ncomb)]
    else:
        assert pos0 % NSA_CMP_STRIDE == 0 and t < NSA_CMP_STRIDE
        npg = page_table.shape[1]
        npp = _pick(npg, 16, 1)
        nst = npg // npp
        cps = npp * (PAGE_SIZE // NSA_CMP_STRIDE)
        n_chunk = nst * cps
        pt = page_table.reshape(-1)
        pool4 = pool.reshape(pool.shape[0], pool.shape[1], PAGE_SIZE * NSA_CACHE_ROWS, HEAD_DIM)
        srcs = [pool4] * npp
        src_specs = [pl.BlockSpec((None, None, PAGE_SIZE * NSA_CACHE_ROWS, HEAD_DIM),
                                  (lambda j: lambda bi, st, pt_: (layer, pt_[bi * npg + st * npp + j], 0, 0))(j))
                     for j in range(npp)]
    grid_spec = pltpu.PrefetchScalarGridSpec(
        num_scalar_prefetch=1,
        grid=(b, nst),
        in_specs=src_specs + [w_spec],
        out_specs=pl.BlockSpec((None, ncomb, cps, NSA_PARTS * HEAD_DIM), lambda bi, st, pt_: (bi, 0, st, 0)),
    )
    return pl.pallas_call(
        functools.partial(_nsa_cmp_kernel, n_src=len(srcs), paged=pool is not None, cps=cps),
        grid_spec=grid_spec,
        out_shape=jax.ShapeDtypeStruct((b, ncomb, n_chunk, NSA_PARTS * HEAD_DIM), F32),
        compiler_params=_cparams("parallel", "parallel"),
    )(pt, *srcs, wc)


def _topk_mask(score, lane, k):
    sel = jnp.zeros(score.shape, jnp.bool_)
    width = score.shape[-1]
    for _ in range(k):
        mx = jnp.max(score, axis=-1, keepdims=True)
        idx = jnp.min(jnp.where(score == mx, lane, width), axis=-1, keepdims=True)
        hit = lane == idx
        sel = jnp.logical_or(sel, jnp.logical_and(hit, mx > NEG_INF / 2))
        score = jnp.where(hit, -3e38, score)
    return sel


def _nsa_select_kernel(q_ref, pr_ref, bc_ref, bias_ref, imap_ref, oc_ref, sel_ref, *, tq, pos0, n_cmp, n_top):
    qt = pl.program_id(1)
    grp, hpg = NSA_KV_HEADS, NSA_HPG
    ncp, nsp = pr_ref.shape[1], imap_ref.shape[1]
    rg = hpg * tq
    t0 = pos0 + qt * tq
    t_row = t0 + lax.rem(lax.broadcasted_iota(jnp.int32, (rg, ncp), 0), tq)
    cidx = lax.broadcasted_iota(jnp.int32, (rg, ncp), 1)
    mask = jnp.logical_and(cidx * NSA_CMP_STRIDE + (NSA_CMP_BLOCK - 1) <= t_row, cidx < n_cmp)
    ti = t0 + lax.broadcasted_iota(jnp.int32, (tq, nsp), 0)
    blk = lax.broadcasted_iota(jnp.int32, (tq, nsp), 1)
    tb = jnp.right_shift(ti, NSA_SEL_SHIFT)
    forced = jnp.logical_or(blk == 0, jnp.logical_or(blk == tb, blk == tb - 1))
    for g in range(grp):
        pk, pv = pr_ref[g], pr_ref[grp + g]
        kc = bc_ref[0:1] + pk[:, 0:HEAD_DIM] + pltpu.roll(pk[:, HEAD_DIM:2 * HEAD_DIM], ncp - 1, axis=0)
        vc = bc_ref[1:2] + pv[:, 0:HEAD_DIM] + pltpu.roll(pv[:, HEAD_DIM:2 * HEAD_DIM], ncp - 1, axis=0)
        qg = (_group_rows(q_ref, g, hpg) * NSA_SCALE).astype(BF16)
        lg = _dot_nt(qg, kc.astype(BF16)) + bias_ref[g * rg:(g + 1) * rg, :]
        z = jnp.where(mask, lg, NEG_INF)
        z = z - jnp.max(z, axis=-1, keepdims=True)
        e = jnp.where(mask, jnp.exp(z), 0.0)
        pc = e / jnp.maximum(jnp.sum(e, axis=-1, keepdims=True), 1e-30)
        oc = _dot(pc.astype(BF16), vc.astype(BF16))
        psum = pc[0:tq]
        for j in range(hpg):
            hh = g * hpg + j
            oc_ref[:, hh * HEAD_DIM:(hh + 1) * HEAD_DIM] = oc[j * tq:(j + 1) * tq]
            if j:
                psum = psum + pc[j * tq:(j + 1) * tq]
        imp = _dot3(psum, imap_ref[...])
        score = jnp.where(blk <= tb, jnp.where(forced, NSA_FORCE, imp), NEG_INF)
        sel_ref[:, g * nsp:(g + 1) * nsp] = _topk_mask(score, blk, n_top).astype(F32)


def _nsa_select(p, proj, b, t, pos0, rel_bias, b_cmp, tq):
    h, grp = N_HEADS, NSA_KV_HEADS
    nq = t // tq
    ncp = proj.shape[2]
    seq_len = pos0 + t
    n_cmp = seq_len // NSA_CMP_STRIDE - NSA_PARTS + 1
    n_sel = -(-seq_len // NSA_SEL_BLOCK)
    nsp = -(-n_sel // LANES) * LANES
    assert n_cmp <= ncp
    c0 = np.arange(ncp)[:, None] * NSA_CMP_STRIDE
    s0 = np.arange(nsp)[None, :] * NSA_SEL_BLOCK
    inter = np.minimum(c0 + NSA_CMP_BLOCK, s0 + NSA_SEL_BLOCK) - np.maximum(c0, s0)
    imap = np.maximum(inter, 0) / NSA_CMP_BLOCK
    imap = np.where((np.arange(ncp)[:, None] < n_cmp) & (np.arange(nsp)[None, :] < n_sel), imap, 0.0)
    imap = jnp.asarray(imap, F32).astype(BF16)
    tpos = pos0 + jnp.arange(nq * tq, dtype=jnp.int32).reshape(nq, tq)
    d = tpos[:, :, None] - (jnp.arange(ncp, dtype=jnp.int32) * NSA_CMP_STRIDE + (NSA_CMP_BLOCK - 1))[None, None, :]
    bias = _bias_lookup(rel_bias, d).transpose(0, 3, 1, 2).reshape(nq, h * tq, ncp)
    return pl.pallas_call(
        functools.partial(_nsa_select_kernel, tq=tq, pos0=pos0, n_cmp=n_cmp, n_top=min(NSA_N_SEL, n_sel)),
        grid=(b, nq),
        in_specs=[pl.BlockSpec((tq, h * HEAD_DIM), lambda bi, qt: (bi * nq + qt, 0)),
                  pl.BlockSpec((None, 2 * grp, ncp, NSA_PARTS * HEAD_DIM), lambda bi, qt: (bi, 0, 0, 0)),
                  pl.BlockSpec((2, HEAD_DIM), lambda bi, qt: (0, 0)),
                  pl.BlockSpec((None, h * tq, ncp), lambda bi, qt: (qt, 0, 0)),
                  pl.BlockSpec((ncp, nsp), lambda bi, qt: (0, 0))],
        out_specs=[pl.BlockSpec((tq, h * HEAD_DIM), lambda bi, qt: (bi * nq + qt, 0)),
                   pl.BlockSpec((tq, grp * nsp), lambda bi, qt: (bi * nq + qt, 0))],
        out_shape=[jax.ShapeDtypeStruct((b * t, h * HEAD_DIM), F32),
                   jax.ShapeDtypeStruct((b * t, grp * nsp), F32)],
        compiler_params=_cparams("parallel", "parallel"),
    )(p, proj, b_cmp, bias, imap)


def _softmax_rows(s, mask, m_ref, l_ref, rows):
    z = jnp.where(mask, s, NEG_INF) if mask is not None else s
    m_old = m_ref[rows, :]
    m_new = jnp.maximum(m_old, jnp.max(z, axis=-1, keepdims=True))
    p = jnp.exp(z - m_new)
    if mask is not None:
        p = jnp.where(mask, p, 0.0)
    alpha = jnp.exp(m_old - m_new)
    l_ref[rows, :] = alpha * l_ref[rows, :] + jnp.sum(p, axis=-1, keepdims=True)
    m_ref[rows, :] = m_new
    return alpha, p


def _fa_tile(qt, kt, *, tq, kb, n_main, pos0, k_base, rel_window):
    t0 = pos0 + qt * tq
    tile = (t0 - k_base) // kb - (n_main - 1) + kt if rel_window else kt
    k0 = k_base + tile * kb
    active = jnp.logical_and(tile >= 0, k0 <= t0 + tq - 1)
    return t0, tile, k0, active


def _nsa_fa_kernel(pt_ref, *refs, kind, kv_mode, tq, kb, n_main, pos0, k_base, rel_window, tail, ts, win_lo, nsp, tail_blk):
    grp, hpg = NSA_KV_HEADS, NSA_HPG
    it = iter(refs)
    q_ref = next(it)
    if kind == 'sel':
        sel_ref, e_ref = next(it), next(it)
    n_kv = {'cols': 2 * grp, 'paged': kb // PAGE_SIZE, 'buf': 1}[kv_mode]
    kv = [next(it) for _ in range(n_kv)]
    tl = [next(it) for _ in range(2 * grp)] if tail else None
    bias_ref = next(it)
    bt_ref = next(it) if tail else None
    o_ref, m_ref, l_ref, acc_ref = next(it), next(it), next(it), next(it)
    qt, kt = pl.program_id(1), pl.program_id(2)
    t0, _, k0, active = _fa_tile(qt, kt, tq=tq, kb=kb, n_main=n_main, pos0=pos0, k_base=k_base, rel_window=rel_window)
    rg = hpg * tq

    @pl.when(kt == 0)
    def _():
        m_ref[...] = jnp.full_like(m_ref, NEG_INF)
        l_ref[...] = jnp.zeros_like(l_ref)
        acc_ref[...] = jnp.zeros_like(acc_ref)

    def load(g, which):
        if kv_mode == 'cols':
            return kv[which * grp + g][...]
        if kv_mode == 'paged':
            off = (2 + which) * grp + g
            return jnp.concatenate([r[pl.ds(off, PAGE_SIZE, stride=NSA_CACHE_ROWS), :] for r in kv], axis=0)
        return kv[0][pl.ds(which * grp + g, kb, stride=NSA_WIN_ROWS), :]

    def base_mask(first_key, nk):
        ti = t0 + lax.rem(lax.broadcasted_iota(jnp.int32, (rg, nk), 0), tq)
        kj = first_key + lax.broadcasted_iota(jnp.int32, (rg, nk), 1)
        m = kj <= ti
        if kind == 'win':
            m = jnp.logical_and(m, jnp.logical_and(ti - kj <= NSA_WINDOW, kj >= win_lo))
        return m

    def attend(g, kg, vg, bias, m):
        rows = slice(g * rg, (g + 1) * rg)
        qg = (_group_rows(q_ref, g, hpg) * NSA_SCALE).astype(BF16)
        s = _dot_nt(qg, kg.astype(BF16)) + bias
        alpha, p = _softmax_rows(s, m, m_ref, l_ref, rows)
        acc_ref[rows, :] = alpha * acc_ref[rows, :] + _dot(p.astype(BF16), vg.astype(BF16))

    def main_tile(masked):
        base = base_mask(k0, kb) if masked else None
        for g in range(grp):
            m = base
            if kind == 'sel':
                ex = _dot(sel_ref[:, g * nsp:(g + 1) * nsp].astype(BF16), e_ref[...])
                chosen = jnp.concatenate([ex] * hpg, axis=0) > 0.5
                m = chosen if base is None else jnp.logical_and(base, chosen)
            attend(g, load(g, 0), load(g, 1), bias_ref[g * rg:(g + 1) * rg, :], m)

    interior = k0 + kb - 1 <= t0
    if kind == 'win':
        interior = jnp.logical_and(interior, jnp.logical_and(t0 + tq - 1 - k0 <= NSA_WINDOW, k0 >= win_lo))

    @pl.when(jnp.logical_and(active, interior))
    def _():
        main_tile(False)

    @pl.when(jnp.logical_and(active, jnp.logical_not(interior)))
    def _():
        main_tile(True)

    @pl.when(kt == n_main - 1)
    def _():
        if tail:
            base = base_mask(pos0, LANES)
            zpad = jnp.zeros((LANES - ts, HEAD_DIM), F32)
            for g in range(grp):
                m = base
                if kind == 'sel':
                    col = sel_ref[:, g * nsp + tail_blk:g * nsp + tail_blk + 1]
                    m = jnp.logical_and(m, jnp.concatenate([col] * hpg, axis=0) > 0.5)
                kg = jnp.concatenate([tl[g][...], zpad], axis=0)
                vg = jnp.concatenate([tl[grp + g][...], zpad], axis=0)
                attend(g, kg, vg, bt_ref[g * rg:(g + 1) * rg, :], m)
        o = acc_ref[...] / jnp.maximum(l_ref[...], 1e-30)
        for hh in range(N_HEADS):
            o_ref[:, hh * HEAD_DIM:(hh + 1) * HEAD_DIM] = o[hh * tq:(hh + 1) * tq]


def _nsa_fa(kind, p, sel, b, t, pos0, rel_bias, pool, layer, page_table, win_buf, tq):
    h, grp = N_HEADS, NSA_KV_HEADS
    nq = t // tq
    kinds = (2, 3) if kind == 'sel' else (4, 5)
    sample = page_table is not None
    seq_len = pos0 + t
    nsp = -(-(-(-seq_len // NSA_SEL_BLOCK)) // LANES) * LANES
    if not sample:
        kb = 2 * tq
        assert t % kb == 0 and NSA_WINDOW % kb == 0
        n_main = t // kb if kind == 'sel' else NSA_WINDOW // kb + 1
        cfg = dict(tq=tq, kb=kb, n_main=n_main, pos0=0, k_base=0, rel_window=kind == 'win')
        cs = (0, tq, 2 * tq, 3 * tq)
        kv_mode, tail, tail_blk, win_lo = 'cols', False, 0, 0
        ntile = t // kb
        pt = jnp.zeros((1,), jnp.int32)
    else:
        assert t <= NSA_SEL_BLOCK and pos0 % NSA_SEL_BLOCK == 0
        tail, tail_blk = True, pos0 // NSA_SEL_BLOCK
        pt = page_table.reshape(-1)
        npg = page_table.shape[1]
        if kind == 'sel':
            kb = _pick(pos0, 1024, PAGE_SIZE)
            cfg = dict(tq=tq, kb=kb, n_main=pos0 // kb, pos0=pos0, k_base=0, rel_window=False)
            kv_mode, win_lo = 'paged', 0
        else:
            kb = win_buf.shape[1]
            cfg = dict(tq=tq, kb=kb, n_main=1, pos0=pos0, k_base=pos0 - kb, rel_window=False)
            kv_mode, win_lo = 'buf', pos0 - kb
        cs = (kb,)
        ntile = cfg['n_main']
    n_main = cfg['n_main']

    def tile_c(qt, kt):
        t0, tile, _, _ = _fa_tile(qt, kt, **cfg)
        return jnp.clip(tile, 0, jnp.minimum((t0 + tq - 1 - cfg['k_base']) // kb, ntile - 1))

    def bias_idx(bi, qt, kt, pt_):
        t0, _, k0, _ = _fa_tile(qt, kt, **cfg)
        idx = 0
        for i, c in enumerate(cs):
            idx = idx + jnp.where(t0 - k0 == c, i + 1, 0)
        return (idx, 0, 0)

    in_specs = [pl.BlockSpec((tq, h * HEAD_DIM), lambda bi, qt, kt, pt_: (bi * nq + qt, 0))]
    args = [p]
    if kind == 'sel':
        e = (np.arange(nsp)[None, :, None] == ((np.arange(ntile)[:, None, None] * kb + np.arange(kb)[None, None, :])
                                                 // NSA_SEL_BLOCK))
        in_specs += [pl.BlockSpec((tq, grp * nsp), lambda bi, qt, kt, pt_: (bi * nq + qt, 0)),
                     pl.BlockSpec((None, nsp, kb), lambda bi, qt, kt, pt_: (tile_c(qt, kt), 0, 0))]
        args += [sel, jnp.asarray(e, F32).astype(BF16)]
    if kv_mode == 'cols':
        for kd in kinds:
            for g in range(grp):
                col = NSA_KV_COL0 + kd * grp + g
                in_specs.append(pl.BlockSpec((kb, HEAD_DIM), (lambda col: lambda bi, qt, kt, pt_: (bi * ntile + tile_c(qt, kt), col))(col)))
                args.append(p)
    elif kv_mode == 'paged':
        ppt = kb // PAGE_SIZE
        pool4 = pool.reshape(pool.shape[0], pool.shape[1], PAGE_SIZE * NSA_CACHE_ROWS, HEAD_DIM)
        for j in range(ppt):
            in_specs.append(pl.BlockSpec((None, None, PAGE_SIZE * NSA_CACHE_ROWS, HEAD_DIM),
                                         (lambda j: lambda bi, qt, kt, pt_: (layer, pt_[bi * npg + kt * ppt + j], 0, 0))(j)))
            args.append(pool4)
    else:
        in_specs.append(pl.BlockSpec((None, kb * NSA_WIN_ROWS, HEAD_DIM), lambda bi, qt, kt, pt_: (bi, 0, 0)))
        args.append(win_buf.reshape(b, kb * NSA_WIN_ROWS, HEAD_DIM))
    if tail:
        for kd in kinds:
            for g in range(grp):
                col = NSA_KV_COL0 + kd * grp + g
                in_specs.append(pl.BlockSpec((t, HEAD_DIM), (lambda col: lambda bi, qt, kt, pt_: (bi, col))(col)))
                args.append(p)
    in_specs.append(pl.BlockSpec((None, h * tq, kb), bias_idx))
    args.append(jnp.stack([_far_bias(rel_bias, tq, kb)] + [_toeplitz_bias(rel_bias, tq, kb, c) for c in cs]))
    if tail:
        in_specs.append(pl.BlockSpec((h * tq, LANES), lambda bi, qt, kt, pt_: (0, 0)))
        args.append(_toeplitz_bias(rel_bias, tq, LANES, 0))
    grid_spec = pltpu.PrefetchScalarGridSpec(
        num_scalar_prefetch=1,
        grid=(b, nq, n_main),
        in_specs=in_specs,
        out_specs=pl.BlockSpec((tq, h * HEAD_DIM), lambda bi, qt, kt, pt_: (bi * nq + qt, 0)),
        scratch_shapes=[pltpu.VMEM((h * tq, 1), F32), pltpu.VMEM((h * tq, 1), F32), pltpu.VMEM((h * tq, HEAD_DIM), F32)],
    )
    return pl.pallas_call(
        functools.partial(_nsa_fa_kernel, kind=kind, kv_mode=kv_mode, tail=tail, ts=t, win_lo=win_lo, nsp=nsp,
                          tail_blk=tail_blk, **cfg),
        grid_spec=grid_spec,
        out_shape=jax.ShapeDtypeStruct((b * t, h * HEAD_DIM), F32),
        compiler_params=_cparams("parallel", "parallel", "arbitrary"),
    )(pt, *args)


def _nsa_merge_kernel(gl_ref, bg_ref, oc_ref, os_ref, ow_ref, o_ref):
    gates = _sigmoid(gl_ref[...] + bg_ref[...])
    for hh in range(N_HEADS):
        cols = slice(hh * HEAD_DIM, (hh + 1) * HEAD_DIM)
        o = (gates[:, 3 * hh:3 * hh + 1] * oc_ref[:, cols] + gates[:, 3 * hh + 1:3 * hh + 2] * os_ref[:, cols]
             + gates[:, 3 * hh + 2:3 * hh + 3] * ow_ref[:, cols])
        o_ref[:, cols] = o.astype(o_ref.dtype)


def _nsa_merge(p, b_gate, o_c, o_s, o_w):
    m = p.shape[0]
    d = N_HEADS * HEAD_DIM
    tm = _pick(m, 256, SUBLANES_BF16)
    gcol = (d + 6 * NSA_KV_HEADS * HEAD_DIM) // LANES
    o_spec = pl.BlockSpec((tm, d), lambda i: (i, 0))
    return pl.pallas_call(
        _nsa_merge_kernel,
        grid=(m // tm,),
        in_specs=[pl.BlockSpec((tm, LANES), lambda i: (i, gcol)), pl.BlockSpec((1, LANES), lambda i: (0, 0)),
                  o_spec, o_spec, o_spec],
        out_specs=o_spec,
        out_shape=jax.ShapeDtypeStruct((m, d), BF16),
        compiler_params=_cparams("parallel"),
    )(p, _pad_cols(b_gate.reshape(1, -1)), o_c, o_s, o_w)


def _nsa_layer(x, b, t, pos0, rel_bias, pool, layer, win_buf, page_table, norm_mix, w_in, b_gate, w_cmp, b_cmp, w_o):
    nq, nkv = N_HEADS * HEAD_DIM, 6 * NSA_KV_HEADS * HEAD_DIM
    p = _mm(x, _pad_cols(w_in, 2 * LANES).astype(BF16), norm_g=norm_mix)
    tq = t if pool is not None else 128
    proj = _nsa_compress(p, b, t, pos0, pool, layer, page_table, w_cmp)
    o_c, sel = _nsa_select(p, proj, b, t, pos0, rel_bias, b_cmp, tq)
    o_s = _nsa_fa('sel', p, sel, b, t, pos0, rel_bias, pool, layer, page_table, win_buf, tq)
    o_w = _nsa_fa('win', p, sel, b, t, pos0, rel_bias, pool, layer, page_table, win_buf, tq)
    o = _nsa_merge(p, b_gate, o_c, o_s, o_w)
    kv = p[:, nq:nq + nkv].reshape(b, t, 6, NSA_KV_HEADS, HEAD_DIM)
    if pool is None:
        win_out = kv[:, t - min(NSA_WINDOW, t):, 4:]
    else:
        win_out = jnp.concatenate([win_buf, kv[:, :, 4:]], axis=1)[:, -win_buf.shape[1]:]
    return _mm(o, w_o.astype(BF16), res=x), kv[:, :, :4], win_out


def _ffn_layer(x, b, t, state, norm_ffn, w_in, conv_w, conv_b, w_out):
    gu = _mm(x, w_in.astype(BF16), norm_g=norm_ffn)
    a = _ffn_gate(gu, conv_w, conv_b, t, state)
    y = _mm(a, w_out.astype(BF16), res=x)
    g_tail = gu.reshape(b, t, 2 * D_FF)[:, max(t - (CONV_W - 1), 0):, :D_FF]
    if state is None:
        prev = jnp.zeros((b, CONV_W - 1, D_FF), F32)
    else:
        prev = state
    new_state = jnp.concatenate([prev, g_tail], axis=1)[:, -(CONV_W - 1):]
    return y, new_state


def kernel(x_prompt, x_sample, state_gla, cache_nsa_kv, cache_nsa_win, cache_mla, cache_moba_kv, state_ffn_conv, page_table, rel_bias, norm_mix, norm_ffn, norm_final, ffn_w_in, ffn_conv_w, ffn_conv_b, ffn_w_out, gla_w_in, gla_b_r, gla_w_a1, gla_w_a2, gla_b_a, gla_norm, gla_w_o, nsa_w_in, nsa_b_gate, nsa_w_cmp, nsa_b_cmp, nsa_w_o, mla_w_down, mla_q_norm, mla_w_uq, mla_kv_norm, mla_w_uk, mla_w_uv, mla_w_o, moba_w_in, moba_w_o):
    bp, tp, d = x_prompt.shape
    bs, ts, _ = x_sample.shape
    past = page_table.shape[1] * PAGE_SIZE
    pos_p = jnp.arange(tp, dtype=jnp.int32)
    pos_s = past + jnp.arange(ts, dtype=jnp.int32)
    xp = x_prompt.reshape(bp * tp, d)
    xs = x_sample.reshape(bs * ts, d)
    outs = {k: [] for k in ('gla_p', 'gla_s', 'nkv_p', 'nkv_s', 'nwin_p', 'nwin_s', 'mla_p', 'mla_s', 'mkv_p', 'mkv_s',
                            'conv_p', 'conv_s')}
    for i in range(DEPTH):
        m, j = i % N_MIXERS, i // N_MIXERS
        if m == 0:
            w = (norm_mix[i], gla_w_in[j], gla_b_r[j], gla_w_a1[j], gla_w_a2[j], gla_b_a[j], gla_norm[j], gla_w_o[j])
            xp, st = _gla_layer(xp, bp, tp, None, *w)
            outs['gla_p'].append(st)
            xs, st = _gla_layer(xs, bs, ts, state_gla[j], *w)
            outs['gla_s'].append(st)
        elif m == 1:
            w = (norm_mix[i], nsa_w_in[j], nsa_b_gate[j], nsa_w_cmp[j], nsa_b_cmp[j], nsa_w_o[j])
            xp, kv, win = _nsa_layer(xp, bp, tp, 0, rel_bias, None, j, None, None, *w)
            outs['nkv_p'].append(kv)
            outs['nwin_p'].append(win)
            xs, kv, win = _nsa_layer(xs, bs, ts, past, rel_bias, cache_nsa_kv, j, cache_nsa_win[j], page_table, *w)
            outs['nkv_s'].append(kv)
            outs['nwin_s'].append(win)
        elif m == 2:
            w = (norm_mix[i], mla_w_down[j], mla_q_norm[j], mla_w_uq[j], mla_kv_norm[j], mla_w_uk[j], mla_w_uv[j],
                 mla_w_o[j])
            xp, lat = _mla_layer(xp, bp, tp, pos_p, None, j, None, *w)
            outs['mla_p'].append(lat)
            xs, lat = _mla_layer(xs, bs, ts, pos_s, cache_mla, j, page_table, *w)
            outs['mla_s'].append(lat)
        else:
            w = (norm_mix[i], moba_w_in[j], moba_w_o[j])
            xp, kv = _moba_layer(xp, bp, tp, 0, rel_bias, None, j, None, *w)
            outs['mkv_p'].append(kv)
            xs, kv = _moba_layer(xs, bs, ts, past, rel_bias, cache_moba_kv, j, page_table, *w)
            outs['mkv_s'].append(kv)
        wf = (norm_ffn[i], ffn_w_in[i], ffn_conv_w[i], ffn_conv_b[i], ffn_w_out[i])
        xp, cst = _ffn_layer(xp, bp, tp, None, *wf)
        outs['conv_p'].append(cst)
        xs, cst = _ffn_layer(xs, bs, ts, state_ffn_conv[i], *wf)
        outs['conv_s'].append(cst)
    y_prompt = _rms(xp, norm_final).reshape(bp, tp, d)
    y_sample = _rms(xs, norm_final).reshape(bs, ts, d)
    return (y_prompt, y_sample) + tuple(jnp.stack(outs[k]) for k in (
        'gla_p', 'gla_s', 'nkv_p', 'nkv_s', 'nwin_p', 'nwin_s', 'mla_p', 'mla_s', 'mkv_p', 'mkv_s', 'conv_p', 'conv_s'))
```
